```python
import jax, jax.numpy as jnp
from jax import lax
import numpy as np

D_MODEL = 2048
BATCH = 4
SEQ = 2048
DEPTH = 2
DEC_BATCH = 128
DEC_SEQ = 8
PAST_LEN = 16384
PAGE_SIZE = 128

D_A = D_MODEL
D_B = D_MODEL
CONV_A_WIDTH = 3
CONV_B_WIDTH = 31
D_FF = ((8 * D_MODEL // 3 + 255) // 256) * 256
RMS_EPS = 1e-6
LN_EPS = 1e-5
IN_SPLITS = [D_A, 2 * D_A, 3 * D_A, 3 * D_A + D_B, 3 * D_A + 2 * D_B, 3 * D_A + 2 * D_B + D_MODEL]
N_IN = 3 * D_A + 2 * D_B + 2 * D_MODEL

kernel_name = "hybrid_shortconv_conformerconv_decode_step"


def _rmsnorm(x, g):
    xf = x.astype(jnp.float32)
    y = xf * lax.rsqrt(jnp.mean(xf * xf, axis=-1, keepdims=True) + RMS_EPS)
    return (y * g.astype(jnp.float32)).astype(x.dtype)


def _layernorm(x, g, b):
    xf = x.astype(jnp.float32)
    mu = jnp.mean(xf, axis=-1, keepdims=True)
    xc = xf - mu
    var = jnp.mean(xc * xc, axis=-1, keepdims=True)
    y = xc * lax.rsqrt(var + LN_EPS) * g.astype(jnp.float32) + b.astype(jnp.float32)
    return y.astype(x.dtype)


def _causal_depthwise(x, buf, w):
    width, ch = w.shape
    xp = jnp.concatenate([buf.astype(x.dtype), x], axis=1)
    y = lax.conv_general_dilated(
        xp, w.astype(x.dtype)[:, None, :], window_strides=(1,), padding='VALID',
        dimension_numbers=('NWC', 'WIO', 'NWC'), feature_group_count=ch)
    return y, xp[:, xp.shape[1] - (width - 1):]


def _layer(x, buf_a, buf_b, norm_mix_g, w_in, conv_a_w, w_out_a, conv_b_w, conv_b_bias,
           ln_b_g, ln_b_b, w_out_b, w_o, norm_ffn_g, w_gate, w_up, w_down):
    h = _rmsnorm(x, norm_mix_g)
    proj = jnp.einsum('btd,dn->btn', h, w_in)
    b_a, c_a, v_a, glu_a, glu_b, gate_a, gate_b = jnp.split(proj, IN_SPLITS, axis=-1)
    conv_a, new_a = _causal_depthwise(c_a * v_a, buf_a, conv_a_w)
    y_a = jnp.einsum('btc,cd->btd', b_a * conv_a, w_out_a)
    u = glu_a * jax.nn.sigmoid(glu_b)
    conv_b, new_b = _causal_depthwise(u, buf_b, conv_b_w)
    z = jax.nn.silu(_layernorm(conv_b + conv_b_bias, ln_b_g, ln_b_b))
    y_b = jnp.einsum('btc,cd->btd', z, w_out_b)
    merged = jax.nn.sigmoid(gate_a) * y_a + jax.nn.sigmoid(gate_b) * y_b
    x = x + jnp.einsum('btd,de->bte', merged, w_o)
    h2 = _rmsnorm(x, norm_ffn_g)
    ff = jax.nn.silu(jnp.einsum('btd,df->btf', h2, w_gate)) * jnp.einsum('btd,df->btf', h2, w_up)
    x = x + jnp.einsum('btf,fd->btd', ff, w_down)
    return x, new_a, new_b


def setup_inputs(seed: int = 0) -> dict:
    key = jax.random.key(seed)
    ks = jax.random.split(key, 20)
    f32 = jnp.float32
    nrm = lambda k, shape, scale: jax.random.normal(k, shape, f32) * scale
    return {
        "x_prompt": nrm(ks[0], (BATCH, SEQ, D_MODEL), 1.0),
        "x_sample": nrm(ks[1], (DEC_BATCH, DEC_SEQ, D_MODEL), 1.0),
        "state_conv_a": nrm(ks[2], (DEPTH, DEC_BATCH, CONV_A_WIDTH - 1, D_A), 0.5),
        "state_conv_b": nrm(ks[3], (DEPTH, DEC_BATCH, CONV_B_WIDTH - 1, D_B), 0.5),
        "norm_mix_g": 1.0 + nrm(ks[4], (DEPTH, D_MODEL), 0.02),
        "w_in": nrm(ks[5], (DEPTH, D_MODEL, N_IN), D_MODEL ** -0.5),
        "conv_a_w": nrm(ks[6], (DEPTH, CONV_A_WIDTH, D_A), CONV_A_WIDTH ** -0.5),
        "w_out_a": nrm(ks[7], (DEPTH, D_A, D_MODEL), D_A ** -0.5),
        "conv_b_w": nrm(ks[8], (DEPTH, CONV_B_WIDTH, D_B), CONV_B_WIDTH ** -0.5),
        "conv_b_bias": nrm(ks[9], (DEPTH, D_B), 0.02),
        "ln_b_g": 1.0 + nrm(ks[10], (DEPTH, D_B), 0.02),
        "ln_b_b": nrm(ks[11], (DEPTH, D_B), 0.02),
        "w_out_b": nrm(ks[12], (DEPTH, D_B, D_MODEL), D_B ** -0.5),
        "w_o": nrm(ks[13], (DEPTH, D_MODEL, D_MODEL), D_MODEL ** -0.5),
        "norm_ffn_g": 1.0 + nrm(ks[14], (DEPTH, D_MODEL), 0.02),
        "w_gate": nrm(ks[15], (DEPTH, D_MODEL, D_FF), D_MODEL ** -0.5),
        "w_up": nrm(ks[16], (DEPTH, D_MODEL, D_FF), D_MODEL ** -0.5),
        "w_down": nrm(ks[17], (DEPTH, D_FF, D_MODEL), D_FF ** -0.5),
        "final_norm_g": 1.0 + nrm(ks[18], (D_MODEL,), 0.02),
    }


def reference(x_prompt, x_sample, state_conv_a, state_conv_b, norm_mix_g, w_in, conv_a_w, w_out_a,
              conv_b_w, conv_b_bias, ln_b_g, ln_b_b, w_out_b, w_o, norm_ffn_g, w_gate, w_up, w_down,
              final_norm_g):
    xp, xs = x_prompt, x_sample
    pa, pb, sa, sb = [], [], [], []
    for l in range(DEPTH):
        params = (norm_mix_g[l], w_in[l], conv_a_w[l], w_out_a[l], conv_b_w[l], conv_b_bias[l],
                  ln_b_g[l], ln_b_b[l], w_out_b[l], w_o[l], norm_ffn_g[l], w_gate[l], w_up[l], w_down[l])
        zero_a = jnp.zeros((xp.shape[0], CONV_A_WIDTH - 1, D_A), xp.dtype)
        zero_b = jnp.zeros((xp.shape[0], CONV_B_WIDTH - 1, D_B), xp.dtype)
        xp, na, nb = _layer(xp, zero_a, zero_b, *params)
        pa.append(na)
        pb.append(nb)
        xs, ma, mb = _layer(xs, state_conv_a[l], state_conv_b[l], *params)
        sa.append(ma)
        sb.append(mb)
    y_prompt = _rmsnorm(xp, final_norm_g)
    y_sample = _rmsnorm(xs, final_norm_g)
    return (y_prompt, y_sample, jnp.stack(pa), jnp.stack(pb), jnp.stack(sa), jnp.stack(sb))
```

```python
import functools

import jax
import jax.numpy as jnp
from jax import lax
from jax.experimental import pallas as pl
from jax.experimental.pallas import tpu as pltpu

D = 2048
N_GROUPS = 7
D_FF = 5632
CONV_A = 3
CONV_B = 31
RMS_EPS = 1e-6
LN_EPS = 1e-5
SUBLANES = 8
HALO_B = 32
HALO_A = 8
VMEM_LIMIT = 56 * 1024 * 1024

BF16 = jnp.bfloat16
F32 = jnp.float32


def _params(*sem):
    return pltpu.CompilerParams(dimension_semantics=sem, vmem_limit_bytes=VMEM_LIMIT)


def _rms(x, g):
    return x * lax.rsqrt(jnp.mean(x * x, axis=-1, keepdims=True) + RMS_EPS) * g


def _in_proj_kernel(x_ref, g_ref, wb, wc, wv, wga, wgb, wta, wtb,
                    ba_o, cv_o, u_o, ga_o, gb_o, h_scr):
    @pl.when(pl.program_id(1) == 0)
    def _():
        h_scr[...] = _rms(x_ref[...], g_ref[...]).astype(BF16)

    h = h_scr[...]

    def dot(w):
        return jnp.dot(h, w[...], preferred_element_type=F32)

    ba_o[...] = dot(wb).astype(BF16)
    cv_o[...] = dot(wc) * dot(wv)
    u_o[...] = dot(wga) * jax.nn.sigmoid(dot(wgb))
    ga_o[...] = jax.nn.sigmoid(dot(wta)).astype(BF16)
    gb_o[...] = jax.nn.sigmoid(dot(wtb)).astype(BF16)


def _in_proj(x, g, w_in, *, tm, tn):
    t = x.shape[0]
    nb = D // tn
    w_specs = [pl.BlockSpec((D, tn), functools.partial(lambda m, n, grp: (0, grp * nb + n), grp=grp))
               for grp in range(N_GROUPS)]
    tile = pl.BlockSpec((tm, tn), lambda m, n: (m, n))
    return pl.pallas_call(
        _in_proj_kernel,
        grid=(t // tm, nb),
        in_specs=[pl.BlockSpec((tm, D), lambda m, n: (m, 0)),
                  pl.BlockSpec((1, D), lambda m, n: (0, 0))] + w_specs,
        out_specs=[tile] * 5,
        out_shape=[jax.ShapeDtypeStruct((t, D), BF16),
                   jax.ShapeDtypeStruct((t, D), F32),
                   jax.ShapeDtypeStruct((t, D), F32),
                   jax.ShapeDtypeStruct((t, D), BF16),
                   jax.ShapeDtypeStruct((t, D), BF16)],
        scratch_shapes=[pltpu.VMEM((tm, D), BF16)],
        compiler_params=_params("arbitrary", "arbitrary"),
        name="in_proj",
    )(x, g, *([w_in] * N_GROUPS))


def _conv_kernel(u_ref, cv_ref, ba_ref, sb_ref, sa_ref, wb8_ref, wb_ref, bias_ref, wa8_ref, wa_ref,
                 cb_o, ya_o, xu_scr, xc_scr, *, tc, n_prompt_tiles, tiles_per_seq, n_batch, rows_per_chunk):
    i = pl.program_id(1)
    bias = bias_ref[...]

    @pl.when(i < n_prompt_tiles)
    def _prompt():
        @pl.when(i % tiles_per_seq == 0)
        def _():
            xu_scr[0:HALO_B, :] = jnp.zeros((HALO_B, xu_scr.shape[1]), F32)
            xc_scr[0:HALO_A, :] = jnp.zeros((HALO_A, xc_scr.shape[1]), F32)

        xu_scr[HALO_B:HALO_B + tc, :] = u_ref[...]
        xc_scr[HALO_A:HALO_A + tc, :] = cv_ref[...]
        groups = rows_per_chunk // SUBLANES
        for c0 in range(0, tc, rows_per_chunk):
            acc = [None] * groups
            for k in range(CONV_B):
                w = wb8_ref[k]
                for gi in range(groups):
                    r = c0 + gi * SUBLANES + HALO_B - (CONV_B - 1) + k
                    term = xu_scr[pl.ds(r, SUBLANES), :] * w
                    acc[gi] = term if acc[gi] is None else acc[gi] + term
            for gi in range(groups):
                cb_o[pl.ds(c0 + gi * SUBLANES, SUBLANES), :] = acc[gi] + bias
            acc = [None] * groups
            for k in range(CONV_A):
                w = wa8_ref[k]
                for gi in range(groups):
                    r = c0 + gi * SUBLANES + HALO_A - (CONV_A - 1) + k
                    term = xc_scr[pl.ds(r, SUBLANES), :] * w
                    acc[gi] = term if acc[gi] is None else acc[gi] + term
            for gi in range(groups):
                rows = pl.ds(c0 + gi * SUBLANES, SUBLANES)
                ya_o[rows, :] = (ba_ref[rows, :].astype(F32) * acc[gi]).astype(BF16)
        xu_scr[0:HALO_B, :] = xu_scr[tc:tc + HALO_B, :]
        xc_scr[0:HALO_A, :] = xc_scr[tc:tc + HALO_A, :]

    @pl.when(i == n_prompt_tiles)
    def _sample():
        n_t = tc // n_batch
        for t in range(n_t):
            rows = pl.ds(t * n_batch, n_batch)
            acc = None
            for k in range(CONV_B):
                j = t + k
                if j < CONV_B - 1:
                    src = sb_ref[j]
                else:
                    src = u_ref[pl.ds((j - (CONV_B - 1)) * n_batch, n_batch), :]
                term = src * wb_ref[k:k + 1, :]
                acc = term if acc is None else acc + term
            cb_o[rows, :] = acc + bias
            acc = None
            for k in range(CONV_A):
                j = t + k
                if j < CONV_A - 1:
                    src = sa_ref[j]
                else:
                    src = cv_ref[pl.ds((j - (CONV_A - 1)) * n_batch, n_batch), :]
                term = src * wa_ref[k:k + 1, :]
                acc = term if acc is None else acc + term
            ya_o[rows, :] = (ba_ref[rows, :].astype(F32) * acc).astype(BF16)


def _convs(u, cv, ba, state_b_tm, state_a_tm, conv_b_w, conv_b_bias, conv_a_w, *, tc, cb, seq, n_prompt, n_batch):
    t = u.shape[0]
    n_prompt_tiles = n_prompt // tc
    assert t - n_prompt == tc and tc % n_batch == 0 and seq % tc == 0
    wb8 = jnp.broadcast_to(conv_b_w[:, None, :], (CONV_B, SUBLANES, D))
    wa8 = jnp.broadcast_to(conv_a_w[:, None, :], (CONV_A, SUBLANES, D))
    tile = pl.BlockSpec((tc, cb), lambda c, i: (i, c))
    kern = functools.partial(_conv_kernel, tc=tc, n_prompt_tiles=n_prompt_tiles, tiles_per_seq=seq // tc,
                             n_batch=n_batch, rows_per_chunk=64)
    return pl.pallas_call(
        kern,
        grid=(D // cb, t // tc),
        in_specs=[tile, tile, tile,
                  pl.BlockSpec((CONV_B - 1, n_batch, cb), lambda c, i: (0, 0, c)),
                  pl.BlockSpec((CONV_A - 1, n_batch, cb), lambda c, i: (0, 0, c)),
                  pl.BlockSpec((CONV_B, SUBLANES, cb), lambda c, i: (0, 0, c)),
                  pl.BlockSpec((CONV_B, cb), lambda c, i: (0, c)),
                  pl.BlockSpec((1, cb), lambda c, i: (0, c)),
                  pl.BlockSpec((CONV_A, SUBLANES, cb), lambda c, i: (0, 0, c)),
                  pl.BlockSpec((CONV_A, cb), lambda c, i: (0, c))],
        out_specs=[tile, tile],
        out_shape=[jax.ShapeDtypeStruct((t, D), F32), jax.ShapeDtypeStruct((t, D), BF16)],
        scratch_shapes=[pltpu.VMEM((HALO_B + tc, cb), F32), pltpu.VMEM((HALO_A + tc, cb), F32)],
        compiler_params=_params("arbitrary", "arbitrary"),
        name="convs",
    )(u, cv, ba, state_b_tm, state_a_tm, wb8, conv_b_w, conv_b_bias, wa8, conv_a_w)


def _merge_kernel(cb_ref, lg_ref, lb_ref, ya_ref, ga_ref, gb_ref, woa_ref, wob_ref, m_o, z_scr):
    @pl.when(pl.program_id(1) == 0)
    def _():
        x = cb_ref[...]
        mu = jnp.mean(x, axis=-1, keepdims=True)
        xc = x - mu
        var = jnp.mean(xc * xc, axis=-1, keepdims=True)
        y = xc * lax.rsqrt(var + LN_EPS) * lg_ref[...] + lb_ref[...]
        z_scr[...] = (y * jax.nn.sigmoid(y)).astype(BF16)

    y_a = jnp.dot(ya_ref[...], woa_ref[...], preferred_element_type=F32)
    y_b = jnp.dot(z_scr[...], wob_ref[...], preferred_element_type=F32)
    m_o[...] = (ga_ref[...].astype(F32) * y_a + gb_ref[...].astype(F32) * y_b).astype(BF16)


def _merge(conv_b, ln_g, ln_b, ya_in, ga, gb, w_out_a, w_out_b, *, tm, tn):
    t = conv_b.shape[0]
    row = pl.BlockSpec((tm, D), lambda m, n: (m, 0))
    vec = pl.BlockSpec((1, D), lambda m, n: (0, 0))
    tile = pl.BlockSpec((tm, tn), lambda m, n: (m, n))
    wcol = pl.BlockSpec((D, tn), lambda m, n: (0, n))
    return pl.pallas_call(
        _merge_kernel,
        grid=(t // tm, D // tn),
        in_specs=[row, vec, vec, row, tile, tile, wcol, wcol],
        out_specs=tile,
        out_shape=jax.ShapeDtypeStruct((t, D), BF16),
        scratch_shapes=[pltpu.VMEM((tm, D), BF16)],
        compiler_params=_params("arbitrary", "arbitrary"),
        name="merge",
    )(conv_b, ln_g, ln_b, ya_in, ga, gb, w_out_a, w_out_b)


def _oproj_kernel(m_ref, w_ref, x_ref, o_ref):
    o_ref[...] = x_ref[...] + jnp.dot(m_ref[...], w_ref[...], preferred_element_type=F32)


def _oproj(merged, w_o, x, *, tm, tn):
    t = x.shape[0]
    tile = pl.BlockSpec((tm, tn), lambda m, n: (m, n))
    return pl.pallas_call(
        _oproj_kernel,
        grid=(t // tm, D // tn),
        in_specs=[pl.BlockSpec((tm, D), lambda m, n: (m, 0)),
                  pl.BlockSpec((D, tn), lambda m, n: (0, n)),
                  tile],
        out_specs=tile,
        out_shape=jax.ShapeDtypeStruct((t, D), F32),
        compiler_params=_params("arbitrary", "arbitrary"),
        name="oproj",
    )(merged, w_o, x)


def _ffn_kernel(x_ref, g_ref, wg_ref, wu_ref, wd_ref, gf_ref, o_ref, h_scr, *, final_norm):
    j = pl.program_id(1)

    @pl.when(j == 0)
    def _():
        x = x_ref[...]
        h_scr[...] = _rms(x, g_ref[...]).astype(BF16)
        o_ref[...] = x

    h = h_scr[...]
    gate = jnp.dot(h, wg_ref[...], preferred_element_type=F32)
    up = jnp.dot(h, wu_ref[...], preferred_element_type=F32)
    act = (gate * jax.nn.sigmoid(gate) * up).astype(BF16)
    o_ref[...] += jnp.dot(act, wd_ref[...], preferred_element_type=F32)

    if final_norm:
        @pl.when(j == pl.num_programs(1) - 1)
        def _():
            o_ref[...] = _rms(o_ref[...], gf_ref[...])


def _ffn(x, g, w_gate, w_up, w_down, g_final, *, tm, tf, final_norm):
    t = x.shape[0]
    row = pl.BlockSpec((tm, D), lambda m, j: (m, 0))
    vec = pl.BlockSpec((1, D), lambda m, j: (0, 0))
    return pl.pallas_call(
        functools.partial(_ffn_kernel, final_norm=final_norm),
        grid=(t // tm, D_FF // tf),
        in_specs=[row, vec,
                  pl.BlockSpec((D, tf), lambda m, j: (0, j)),
                  pl.BlockSpec((D, tf), lambda m, j: (0, j)),
                  pl.BlockSpec((tf, D), lambda m, j: (j, 0)),
                  vec],
        out_specs=row,
        out_shape=jax.ShapeDtypeStruct((t, D), F32),
        scratch_shapes=[pltpu.VMEM((tm, D), BF16)],
        compiler_params=_params("arbitrary", "arbitrary"),
        name="ffn",
    )(x, g, w_gate, w_up, w_down, g_final)


def kernel(x_prompt, x_sample, state_conv_a, state_conv_b, norm_mix_g, w_in, conv_a_w, w_out_a, conv_b_w,
           conv_b_bias, ln_b_g, ln_b_b, w_out_b, w_o, norm_ffn_g, w_gate, w_up, w_down, final_norm_g):
    n_seq, seq, _ = x_prompt.shape
    n_batch, n_dec, _ = x_sample.shape
    depth = w_in.shape[0]
    n_prompt = n_seq * seq
    n_sample = n_batch * n_dec

    x = jnp.concatenate([x_prompt.reshape(n_prompt, D),
                         x_sample.transpose(1, 0, 2).reshape(n_sample, D)], axis=0)
    g_final = final_norm_g.reshape(1, D)
    pa, pb, sa, sb = [], [], [], []
    for l in range(depth):
        bf = lambda w: w[l].astype(BF16)
        vec = lambda v: v[l].reshape(1, D)
        ba, cv, u, ga, gb = _in_proj(x, vec(norm_mix_g), bf(w_in), tm=512, tn=256)
        conv_b, ya_in = _convs(u, cv, ba,
                               state_conv_b[l].transpose(1, 0, 2), state_conv_a[l].transpose(1, 0, 2),
                               conv_b_w[l], vec(conv_b_bias), conv_a_w[l],
                               tc=n_sample, cb=256, seq=seq, n_prompt=n_prompt, n_batch=n_batch)
        merged = _merge(conv_b, vec(ln_b_g), vec(ln_b_b), ya_in, ga, gb, bf(w_out_a), bf(w_out_b), tm=512, tn=512)
        x = _oproj(merged, bf(w_o), x, tm=1024, tn=512)
        x = _ffn(x, vec(norm_ffn_g), bf(w_gate), bf(w_up), bf(w_down), g_final,
                 tm=512, tf=512, final_norm=(l == depth - 1))

        cv_p = cv[:n_prompt].reshape(n_seq, seq, D)
        u_p = u[:n_prompt].reshape(n_seq, seq, D)
        cv_s = cv[n_prompt:].reshape(n_dec, n_batch, D).transpose(1, 0, 2)
        u_s = u[n_prompt:].reshape(n_dec, n_batch, D).transpose(1, 0, 2)
        pa.append(cv_p[:, seq - (CONV_A - 1):])
        pb.append(u_p[:, seq - (CONV_B - 1):])
        sa.append(jnp.concatenate([state_conv_a[l], cv_s], axis=1)[:, n_dec:])
        sb.append(jnp.concatenate([state_conv_b[l], u_s], axis=1)[:, n_dec:])

    y_prompt = x[:n_prompt].reshape(n_seq, seq, D)
    y_sample = x[n_prompt:].reshape(n_dec, n_batch, D).transpose(1, 0, 2)
    return (y_prompt, y_sample, jnp.stack(pa), jnp.stack(pb), jnp.stack(sa), jnp.stack(sb))
```

```python
import functools

import jax
import jax.numpy as jnp
from jax import lax
from jax.experimental import pallas as pl
from jax.experimental.pallas import tpu as pltpu

D = 2048
N_GROUPS = 7
D_FF = 5632
CONV_A = 3
CONV_B = 31
RMS_EPS = 1e-6
LN_EPS = 1e-5
SUBLANES = 8
HIST_B = (CONV_B - 1) * SUBLANES
HIST_A = (CONV_A - 1) * SUBLANES
CONV_CHUNK = 32
VMEM_LIMIT = 56 * 1024 * 1024

BF16 = jnp.bfloat16
F32 = jnp.float32


def _params(*sem):
    return pltpu.CompilerParams(dimension_semantics=sem, vmem_limit_bytes=VMEM_LIMIT)


def _rms(x, g):
    return x * lax.rsqrt(jnp.mean(x * x, axis=-1, keepdims=True) + RMS_EPS) * g


def _in_proj_kernel(x_ref, g_ref, wb, wc, wv, wga, wgb, wta, wtb,
                    ba_o, cv_o, u_o, ga_o, gb_o, h_scr):
    @pl.when(pl.program_id(1) == 0)
    def _():
        h_scr[...] = _rms(x_ref[...], g_ref[...]).astype(BF16)

    h = h_scr[...]

    def dot(w):
        return jnp.dot(h, w[...], preferred_element_type=F32)

    ba_o[...] = dot(wb).astype(BF16)
    cv_o[...] = dot(wc) * dot(wv)
    u_o[...] = dot(wga) * jax.nn.sigmoid(dot(wgb))
    ga_o[...] = jax.nn.sigmoid(dot(wta)).astype(BF16)
    gb_o[...] = jax.nn.sigmoid(dot(wtb)).astype(BF16)


def _in_proj(x, g, w_in, layer, *, tm, tn):
    t = x.shape[0]
    nb = D // tn
    w_specs = [pl.BlockSpec((None, D, tn), functools.partial(lambda m, n, grp: (layer, 0, grp * nb + n), grp=grp))
               for grp in range(N_GROUPS)]
    tile = pl.BlockSpec((tm, tn), lambda m, n: (m, n))
    return pl.pallas_call(
        _in_proj_kernel,
        grid=(t // tm, nb),
        in_specs=[pl.BlockSpec((tm, D), lambda m, n: (m, 0)),
                  pl.BlockSpec((1, D), lambda m, n: (0, 0))] + w_specs,
        out_specs=[tile] * 5,
        out_shape=[jax.ShapeDtypeStruct((t, D), BF16),
                   jax.ShapeDtypeStruct((t, D), F32),
                   jax.ShapeDtypeStruct((t, D), F32),
                   jax.ShapeDtypeStruct((t, D), BF16),
                   jax.ShapeDtypeStruct((t, D), BF16)],
        scratch_shapes=[pltpu.VMEM((tm, D), BF16)],
        compiler_params=_params("arbitrary", "arbitrary"),
        name="in_proj",
    )(x, g, *([w_in] * N_GROUPS))


def _prev_segment(rows):
    moved = pltpu.roll(rows, shift=1, axis=0)
    sub = lax.broadcasted_iota(jnp.int32, rows.shape, 0)
    return jnp.where(sub == 0, 0.0, moved)


def _conv_kernel(u_ref, cv_ref, ba_ref, uw_ref, cw_ref, sb_ref, sa_ref, wb_ref, wb8_ref, bias_ref, wa_ref,
                 cb_o, ya_o, pb_o, pa_o, nsb_o, nsa_o, xu_scr, xc_scr,
                 *, tc, n_prompt_tiles, tiles_per_seq, n_batch):
    i = pl.program_id(1)
    n_chunks = tc // CONV_CHUNK
    groups = CONV_CHUNK // SUBLANES

    @pl.when(i < n_prompt_tiles)
    def _prompt():
        @pl.when(i % tiles_per_seq == 0)
        def _():
            w0 = uw_ref.shape[0] - HIST_B
            for j in range(CONV_B - 1):
                xu_scr[pl.ds(j * SUBLANES, SUBLANES), :] = _prev_segment(
                    uw_ref[pl.ds(w0 + j * SUBLANES, SUBLANES), :])
            w0 = cw_ref.shape[0] - HIST_A
            for j in range(CONV_A - 1):
                xc_scr[pl.ds(j * SUBLANES, SUBLANES), :] = _prev_segment(
                    cw_ref[pl.ds(w0 + j * SUBLANES, SUBLANES), :])

        @pl.when(i % tiles_per_seq != 0)
        def _():
            xu_scr[0:HIST_B, :] = xu_scr[tc:tc + HIST_B, :]
            xc_scr[0:HIST_A, :] = xc_scr[tc:tc + HIST_A, :]

        xu_scr[HIST_B:HIST_B + tc, :] = u_ref[...]
        xc_scr[HIST_A:HIST_A + tc, :] = cv_ref[...]

        bias8 = jnp.broadcast_to(bias_ref[...], (SUBLANES, bias_ref.shape[1]))

        def chunk(ci, carry):
            r0 = pl.multiple_of(ci * CONV_CHUNK, CONV_CHUNK)
            accs = [bias8] * groups
            for k in range(CONV_B):
                w = wb8_ref[k]
                for gi in range(groups):
                    accs[gi] = accs[gi] + xu_scr[pl.ds(r0 + (k + gi) * SUBLANES, SUBLANES), :] * w
            for gi in range(groups):
                cb_o[pl.ds(r0 + gi * SUBLANES, SUBLANES), :] = accs[gi]
            acc = xc_scr[pl.ds(r0, CONV_CHUNK), :] * wa_ref[0:1, :]
            for k in range(1, CONV_A):
                acc = acc + xc_scr[pl.ds(r0 + k * SUBLANES, CONV_CHUNK), :] * wa_ref[k:k + 1, :]
            ya_o[pl.ds(r0, CONV_CHUNK), :] = (ba_ref[pl.ds(r0, CONV_CHUNK), :].astype(F32) * acc).astype(BF16)
            return carry

        lax.fori_loop(0, n_chunks, chunk, 0)

        @pl.when(i % tiles_per_seq == tiles_per_seq - 1)
        def _():
            last = SUBLANES - 1
            for j in range(CONV_B - 1):
                pb_o[pl.ds(j, 1), :] = u_ref[pl.ds(tc - HIST_B + j * SUBLANES + last, 1), :]
            for j in range(CONV_A - 1):
                pa_o[pl.ds(j, 1), :] = cv_ref[pl.ds(tc - HIST_A + j * SUBLANES + last, 1), :]

    @pl.when(i == n_prompt_tiles)
    def _sample():
        n_t = tc // n_batch
        bias8 = jnp.broadcast_to(bias_ref[...], (SUBLANES, bias_ref.shape[1]))
        for t in range(n_t):
            def chunk(ci, carry, t=t):
                r0 = pl.multiple_of(ci * CONV_CHUNK, CONV_CHUNK)
                rows = pl.ds(t * n_batch + r0, CONV_CHUNK)
                accs = [bias8] * groups
                for k in range(CONV_B):
                    j = t + k
                    w = wb8_ref[k]
                    for gi in range(groups):
                        g0 = r0 + gi * SUBLANES
                        if j < CONV_B - 1:
                            src = sb_ref[j, pl.ds(g0, SUBLANES), :]
                        else:
                            src = u_ref[pl.ds((j - (CONV_B - 1)) * n_batch + g0, SUBLANES), :]
                        accs[gi] = accs[gi] + src * w
                for gi in range(groups):
                    cb_o[pl.ds(t * n_batch + r0 + gi * SUBLANES, SUBLANES), :] = accs[gi]
                acc = None
                for k in range(CONV_A):
                    j = t + k
                    if j < CONV_A - 1:
                        src = sa_ref[j, pl.ds(r0, CONV_CHUNK), :]
                    else:
                        src = cv_ref[pl.ds((j - (CONV_A - 1)) * n_batch + r0, CONV_CHUNK), :]
                    term = src * wa_ref[k:k + 1, :]
                    acc = term if acc is None else acc + term
                ya_o[rows, :] = (ba_ref[rows, :].astype(F32) * acc).astype(BF16)
                return carry

            lax.fori_loop(0, n_batch // CONV_CHUNK, chunk, 0)
        for j in range(CONV_B - 1):
            jj = j + n_t
            if jj < CONV_B - 1:
                nsb_o[j] = sb_ref[jj]
            else:
                nsb_o[j] = u_ref[pl.ds((jj - (CONV_B - 1)) * n_batch, n_batch), :]
        for j in range(CONV_A - 1):
            jj = j + n_t
            if jj < CONV_A - 1:
                nsa_o[j] = sa_ref[jj]
            else:
                nsa_o[j] = cv_ref[pl.ds((jj - (CONV_A - 1)) * n_batch, n_batch), :]


def _convs(u, cv, ba, state_b_tm, state_a_tm, conv_b_w, conv_b_bias, conv_a_w, *, tc, cb, n_seq, seq, n_batch):
    t = u.shape[0]
    n_prompt = n_seq * seq
    n_prompt_tiles = n_prompt // tc
    tiles_per_seq = seq // tc
    wrap_b, wrap_a = 256, 16
    assert t - n_prompt == tc and tc % n_batch == 0 and seq % tc == 0
    assert wrap_b >= HIST_B and wrap_a >= HIST_A and tc % wrap_b == 0 and tc % CONV_CHUNK == 0

    def seq_of(i):
        return jnp.minimum(i // tiles_per_seq, n_seq - 1)

    tile = pl.BlockSpec((tc, cb), lambda c, i: (i, c))
    kern = functools.partial(_conv_kernel, tc=tc, n_prompt_tiles=n_prompt_tiles, tiles_per_seq=tiles_per_seq,
                             n_batch=n_batch)
    return pl.pallas_call(
        kern,
        grid=(D // cb, t // tc),
        in_specs=[tile, tile, tile,
                  pl.BlockSpec((wrap_b, cb), lambda c, i: ((seq_of(i) + 1) * (seq // wrap_b) - 1, c)),
                  pl.BlockSpec((wrap_a, cb), lambda c, i: ((seq_of(i) + 1) * (seq // wrap_a) - 1, c)),
                  pl.BlockSpec((CONV_B - 1, n_batch, cb), lambda c, i: (0, 0, c)),
                  pl.BlockSpec((CONV_A - 1, n_batch, cb), lambda c, i: (0, 0, c)),
                  pl.BlockSpec((CONV_B, cb), lambda c, i: (0, c)),
                  pl.BlockSpec((CONV_B, SUBLANES, cb), lambda c, i: (0, 0, c)),
                  pl.BlockSpec((1, cb), lambda c, i: (0, c)),
                  pl.BlockSpec((CONV_A, cb), lambda c, i: (0, c))],
        out_specs=[tile, tile,
                   pl.BlockSpec((None, CONV_B - 1, cb), lambda c, i: (seq_of(i), 0, c)),
                   pl.BlockSpec((None, CONV_A - 1, cb), lambda c, i: (seq_of(i), 0, c)),
                   pl.BlockSpec((CONV_B - 1, n_batch, cb), lambda c, i: (0, 0, c)),
                   pl.BlockSpec((CONV_A - 1, n_batch, cb), lambda c, i: (0, 0, c))],
        out_shape=[jax.ShapeDtypeStruct((t, D), F32), jax.ShapeDtypeStruct((t, D), BF16),
                   jax.ShapeDtypeStruct((n_seq, CONV_B - 1, D), F32),
                   jax.ShapeDtypeStruct((n_seq, CONV_A - 1, D), F32),
                   jax.ShapeDtypeStruct((CONV_B - 1, n_batch, D), F32),
                   jax.ShapeDtypeStruct((CONV_A - 1, n_batch, D), F32)],
        scratch_shapes=[pltpu.VMEM((HIST_B + tc, cb), F32), pltpu.VMEM((HIST_A + tc, cb), F32)],
        compiler_params=_params("arbitrary", "arbitrary"),
        name="convs",
    )(u, cv, ba, u, cv, state_b_tm, state_a_tm, conv_b_w,
      jnp.broadcast_to(conv_b_w[:, None, :], (CONV_B, SUBLANES, D)), conv_b_bias, conv_a_w)


def _merge_kernel(cb_ref, lg_ref, lb_ref, ya_ref, ga_ref, gb_ref, woa_ref, wob_ref, m_o, z_scr):
    @pl.when(pl.program_id(1) == 0)
    def _():
        x = cb_ref[...]
        mu = jnp.mean(x, axis=-1, keepdims=True)
        xc = x - mu
        var = jnp.mean(xc * xc, axis=-1, keepdims=True)
        y = xc * lax.rsqrt(var + LN_EPS) * lg_ref[...] + lb_ref[...]
        z_scr[...] = (y * jax.nn.sigmoid(y)).astype(BF16)

    y_a = jnp.dot(ya_ref[...], woa_ref[...], preferred_element_type=F32)
    y_b = jnp.dot(z_scr[...], wob_ref[...], preferred_element_type=F32)
    m_o[...] = (ga_ref[...].astype(F32) * y_a + gb_ref[...].astype(F32) * y_b).astype(BF16)


def _merge(conv_b, ln_g, ln_b, ya_in, ga, gb, w_out_a, w_out_b, layer, *, tm, tn):
    t = conv_b.shape[0]
    row = pl.BlockSpec((tm, D), lambda m, n: (m, 0))
    vec = pl.BlockSpec((1, D), lambda m, n: (0, 0))
    tile = pl.BlockSpec((tm, tn), lambda m, n: (m, n))
    wcol = pl.BlockSpec((None, D, tn), lambda m, n: (layer, 0, n))
    return pl.pallas_call(
        _merge_kernel,
        grid=(t // tm, D // tn),
        in_specs=[row, vec, vec, row, tile, tile, wcol, wcol],
        out_specs=tile,
        out_shape=jax.ShapeDtypeStruct((t, D), BF16),
        scratch_shapes=[pltpu.VMEM((tm, D), BF16)],
        compiler_params=_params("arbitrary", "arbitrary"),
        name="merge",
    )(conv_b, ln_g, ln_b, ya_in, ga, gb, w_out_a, w_out_b)


def _oproj_kernel(m_ref, w_ref, x_ref, o_ref):
    o_ref[...] = x_ref[...] + jnp.dot(m_ref[...], w_ref[...], preferred_element_type=F32)


def _oproj(merged, w_o, layer, x, *, tm):
    t = x.shape[0]
    row = pl.BlockSpec((tm, D), lambda m: (m, 0))
    return pl.pallas_call(
        _oproj_kernel,
        grid=(t // tm,),
        in_specs=[row, pl.BlockSpec((None, D, D), lambda m: (layer, 0, 0)), row],
        out_specs=row,
        out_shape=jax.ShapeDtypeStruct((t, D), F32),
        compiler_params=_params("arbitrary"),
        name="oproj",
    )(merged, w_o, x)


def _ffn_kernel(x_ref, g_ref, wg_ref, wu_ref, wd_ref, gf_ref, o_ref, h_scr, *, final_norm):
    j = pl.program_id(1)

    @pl.when(j == 0)
    def _():
        x = x_ref[...]
        h_scr[...] = _rms(x, g_ref[...]).astype(BF16)
        o_ref[...] = x

    h = h_scr[...]
    gate = jnp.dot(h, wg_ref[...], preferred_element_type=F32)
    up = jnp.dot(h, wu_ref[...], preferred_element_type=F32)
    act = (gate * jax.nn.sigmoid(gate) * up).astype(BF16)
    o_ref[...] += jnp.dot(act, wd_ref[...], preferred_element_type=F32)

    if final_norm:
        @pl.when(j == pl.num_programs(1) - 1)
        def _():
            o_ref[...] = _rms(o_ref[...], gf_ref[...])


def _ffn(x, g, w_gate, w_up, w_down, layer, g_final, *, tm, tf, final_norm):
    t = x.shape[0]
    row = pl.BlockSpec((tm, D), lambda m, j: (m, 0))
    vec = pl.BlockSpec((1, D), lambda m, j: (0, 0))
    return pl.pallas_call(
        functools.partial(_ffn_kernel, final_norm=final_norm),
        grid=(t // tm, D_FF // tf),
        in_specs=[row, vec,
                  pl.BlockSpec((None, D, tf), lambda m, j: (layer, 0, j)),
                  pl.BlockSpec((None, D, tf), lambda m, j: (layer, 0, j)),
                  pl.BlockSpec((None, tf, D), lambda m, j: (layer, j, 0)),
                  vec],
        out_specs=row,
        out_shape=jax.ShapeDtypeStruct((t, D), F32),
        scratch_shapes=[pltpu.VMEM((tm, D), BF16)],
        compiler_params=_params("arbitrary", "arbitrary"),
        name="ffn",
    )(x, g, w_gate, w_up, w_down, g_final)


def kernel(x_prompt, x_sample, state_conv_a, state_conv_b, norm_mix_g, w_in, conv_a_w, w_out_a, conv_b_w,
           conv_b_bias, ln_b_g, ln_b_b, w_out_b, w_o, norm_ffn_g, w_gate, w_up, w_down, final_norm_g):
    n_seq, seq, _ = x_prompt.shape
    n_batch, n_dec, _ = x_sample.shape
    depth = w_in.shape[0]
    n_prompt = n_seq * seq
    n_sample = n_batch * n_dec
    seg = seq // SUBLANES

    x = jnp.concatenate(
        [x_prompt.reshape(n_seq, SUBLANES, seg, D).transpose(0, 2, 1, 3).reshape(n_prompt, D),
         x_sample.transpose(1, 0, 2).reshape(n_sample, D)], axis=0)
    w_in, w_out_a, w_out_b, w_o, w_gate, w_up, w_down = (
        w.astype(BF16) for w in (w_in, w_out_a, w_out_b, w_o, w_gate, w_up, w_down))
    g_final = final_norm_g.reshape(1, D)
    pa, pb, sa, sb = [], [], [], []
    for l in range(depth):
        vec = lambda v: v[l].reshape(1, D)
        ba, cv, u, ga, gb = _in_proj(x, vec(norm_mix_g), w_in, l, tm=512, tn=256)
        conv_b, ya_in, new_pb, new_pa, new_sb, new_sa = _convs(
            u, cv, ba, state_conv_b[l].transpose(1, 0, 2), state_conv_a[l].transpose(1, 0, 2),
            conv_b_w[l], vec(conv_b_bias), conv_a_w[l],
            tc=n_sample, cb=256, n_seq=n_seq, seq=seq, n_batch=n_batch)
        merged = _merge(conv_b, vec(ln_b_g), vec(ln_b_b), ya_in, ga, gb, w_out_a, w_out_b, l, tm=512, tn=512)
        x = _oproj(merged, w_o, l, x, tm=512)
        x = _ffn(x, vec(norm_ffn_g), w_gate, w_up, w_down, l, g_final,
                 tm=512, tf=512, final_norm=(l == depth - 1))
        pa.append(new_pa)
        pb.append(new_pb)
        sa.append(new_sa.transpose(1, 0, 2))
        sb.append(new_sb.transpose(1, 0, 2))

    y_prompt = x[:n_prompt].reshape(n_seq, seg, SUBLANES, D).transpose(0, 2, 1, 3).reshape(n_seq, seq, D)
    y_sample = x[n_prompt:].reshape(n_dec, n_batch, D).transpose(1, 0, 2)
    return (y_prompt, y_sample, jnp.stack(pa), jnp.stack(pb), jnp.stack(sa), jnp.stack(sb))
```

```python
import functools

import jax
import jax.numpy as jnp
from jax import lax
from jax.experimental import pallas as pl
from jax.experimental.pallas import tpu as pltpu

D = 2048
N_GROUPS = 7
D_FF = 5632
CONV_A = 3
CONV_B = 31
RMS_EPS = 1e-6
LN_EPS = 1e-5
SUBLANES = 8
HIST_B = (CONV_B - 1) * SUBLANES
HIST_A = (CONV_A - 1) * SUBLANES
CONV_CHUNK = 32
VMEM_LIMIT = 56 * 1024 * 1024

BF16 = jnp.bfloat16
F32 = jnp.float32


def _params(*sem):
    return pltpu.CompilerParams(dimension_semantics=sem, vmem_limit_bytes=VMEM_LIMIT)


def _rms(x, g):
    return x * lax.rsqrt(jnp.mean(x * x, axis=-1, keepdims=True) + RMS_EPS) * g


def _in_proj_a_kernel(x_ref, g_ref, wb, wc, wv, ba_o, cv_o, h_scr):
    @pl.when(pl.program_id(1) == 0)
    def _():
        h_scr[...] = _rms(x_ref[...], g_ref[...]).astype(BF16)

    h = h_scr[...]

    def dot(w):
        return jnp.dot(h, w[...].astype(BF16), preferred_element_type=F32)

    ba_o[...] = dot(wb).astype(BF16)
    cv_o[...] = dot(wc) * dot(wv)


def _in_proj_b_kernel(x_ref, g_ref, wga, wgb, wta, wtb, u_o, ga_o, gb_o, h_scr):
    @pl.when(pl.program_id(1) == 0)
    def _():
        h_scr[...] = _rms(x_ref[...], g_ref[...]).astype(BF16)

    h = h_scr[...]

    def dot(w):
        return jnp.dot(h, w[...].astype(BF16), preferred_element_type=F32)

    u_o[...] = dot(wga) * jax.nn.sigmoid(dot(wgb))
    ga_o[...] = jax.nn.sigmoid(dot(wta)).astype(BF16)
    gb_o[...] = jax.nn.sigmoid(dot(wtb)).astype(BF16)


def _in_proj_part(body, name, x, g, w_in, layer, groups, out_dtypes, *, tm, tn):
    t = x.shape[0]
    nb = D // tn
    w_specs = [pl.BlockSpec((None, D, tn), functools.partial(lambda m, n, grp: (layer, 0, grp * nb + n), grp=grp))
               for grp in groups]
    tile = pl.BlockSpec((tm, tn), lambda m, n: (m, n))
    return pl.pallas_call(
        body,
        grid=(t // tm, nb),
        in_specs=[pl.BlockSpec((tm, D), lambda m, n: (m, 0), pipeline_mode=pl.Buffered(1)),
                  pl.BlockSpec((1, D), lambda m, n: (0, 0))] + w_specs,
        out_specs=[tile] * len(out_dtypes),
        out_shape=[jax.ShapeDtypeStruct((t, D), dt) for dt in out_dtypes],
        scratch_shapes=[pltpu.VMEM((tm, D), BF16)],
        compiler_params=_params("arbitrary", "arbitrary"),
        name=name,
    )(x, g, *([w_in] * len(groups)))


def _in_proj(x, g, w_in, layer, *, tm, tn):
    ba, cv = _in_proj_part(_in_proj_a_kernel, "in_proj_a", x, g, w_in, layer, (0, 1, 2), (BF16, F32), tm=tm, tn=tn)
    u, ga, gb = _in_proj_part(_in_proj_b_kernel, "in_proj_b", x, g, w_in, layer, (3, 4, 5, 6), (F32, BF16, BF16),
                              tm=tm, tn=tn)
    return ba, cv, u, ga, gb


def _prev_segment(rows):
    moved = pltpu.roll(rows, shift=1, axis=0)
    sub = lax.broadcasted_iota(jnp.int32, rows.shape, 0)
    return jnp.where(sub == 0, 0.0, moved)


def _conv_kernel(u_ref, cv_ref, ba_ref, uw_ref, cw_ref, sb_ref, sa_ref, wb_ref, wb8_ref, bias_ref, wa_ref,
                 cb_o, ya_o, pb_o, pa_o, nsb_o, nsa_o, xu_scr, xc_scr,
                 *, tc, n_prompt_tiles, tiles_per_seq, n_batch):
    i = pl.program_id(1)
    n_chunks = tc // CONV_CHUNK
    groups = CONV_CHUNK // SUBLANES

    @pl.when(i < n_prompt_tiles)
    def _prompt():
        @pl.when(i % tiles_per_seq == 0)
        def _():
            w0 = uw_ref.shape[0] - HIST_B
            for j in range(CONV_B - 1):
                xu_scr[pl.ds(j * SUBLANES, SUBLANES), :] = _prev_segment(
                    uw_ref[pl.ds(w0 + j * SUBLANES, SUBLANES), :])
            w0 = cw_ref.shape[0] - HIST_A
            for j in range(CONV_A - 1):
                xc_scr[pl.ds(j * SUBLANES, SUBLANES), :] = _prev_segment(
                    cw_ref[pl.ds(w0 + j * SUBLANES, SUBLANES), :])

        @pl.when(i % tiles_per_seq != 0)
        def _():
            xu_scr[0:HIST_B, :] = xu_scr[tc:tc + HIST_B, :]
            xc_scr[0:HIST_A, :] = xc_scr[tc:tc + HIST_A, :]

        xu_scr[HIST_B:HIST_B + tc, :] = u_ref[...]
        xc_scr[HIST_A:HIST_A + tc, :] = cv_ref[...]

        bias8 = jnp.broadcast_to(bias_ref[...], (SUBLANES, bias_ref.shape[1]))

        def chunk(ci, carry):
            r0 = pl.multiple_of(ci * CONV_CHUNK, CONV_CHUNK)
            accs = [bias8] * groups
            for k in range(CONV_B):
                w = wb8_ref[k]
                for gi in range(groups):
                    accs[gi] = accs[gi] + xu_scr[pl.ds(r0 + (k + gi) * SUBLANES, SUBLANES), :] * w
            for gi in range(groups):
                cb_o[pl.ds(r0 + gi * SUBLANES, SUBLANES), :] = accs[gi]
            acc = xc_scr[pl.ds(r0, CONV_CHUNK), :] * wa_ref[0:1, :]
            for k in range(1, CONV_A):
                acc = acc + xc_scr[pl.ds(r0 + k * SUBLANES, CONV_CHUNK), :] * wa_ref[k:k + 1, :]
            ya_o[pl.ds(r0, CONV_CHUNK), :] = (ba_ref[pl.ds(r0, CONV_CHUNK), :].astype(F32) * acc).astype(BF16)
            return carry

        lax.fori_loop(0, n_chunks, chunk, 0)

        @pl.when(i % tiles_per_seq == tiles_per_seq - 1)
        def _():
            last = SUBLANES - 1
            for j in range(CONV_B - 1):
                pb_o[pl.ds(j, 1), :] = u_ref[pl.ds(tc - HIST_B + j * SUBLANES + last, 1), :]
            for j in range(CONV_A - 1):
                pa_o[pl.ds(j, 1), :] = cv_ref[pl.ds(tc - HIST_A + j * SUBLANES + last, 1), :]

    @pl.when(i == n_prompt_tiles)
    def _sample():
        n_t = tc // n_batch
        bias8 = jnp.broadcast_to(bias_ref[...], (SUBLANES, bias_ref.shape[1]))
        for t in range(n_t):
            def chunk(ci, carry, t=t):
                r0 = pl.multiple_of(ci * CONV_CHUNK, CONV_CHUNK)
                rows = pl.ds(t * n_batch + r0, CONV_CHUNK)
                accs = [bias8] * groups
                for k in range(CONV_B):
                    j = t + k
                    w = wb8_ref[k]
                    for gi in range(groups):
                        g0 = r0 + gi * SUBLANES
                        if j < CONV_B - 1:
                            src = sb_ref[j, pl.ds(g0, SUBLANES), :]
                        else:
                            src = u_ref[pl.ds((j - (CONV_B - 1)) * n_batch + g0, SUBLANES), :]
                        accs[gi] = accs[gi] + src * w
                for gi in range(groups):
                    cb_o[pl.ds(t * n_batch + r0 + gi * SUBLANES, SUBLANES), :] = accs[gi]
                acc = None
                for k in range(CONV_A):
                    j = t + k
                    if j < CONV_A - 1:
                        src = sa_ref[j, pl.ds(r0, CONV_CHUNK), :]
                    else:
                        src = cv_ref[pl.ds((j - (CONV_A - 1)) * n_batch + r0, CONV_CHUNK), :]
                    term = src * wa_ref[k:k + 1, :]
                    acc = term if acc is None else acc + term
                ya_o[rows, :] = (ba_ref[rows, :].astype(F32) * acc).astype(BF16)
                return carry

            lax.fori_loop(0, n_batch // CONV_CHUNK, chunk, 0)
        for j in range(CONV_B - 1):
            jj = j + n_t
            if jj < CONV_B - 1:
                nsb_o[j] = sb_ref[jj]
            else:
                nsb_o[j] = u_ref[pl.ds((jj - (CONV_B - 1)) * n_batch, n_batch), :]
        for j in range(CONV_A - 1):
            jj = j + n_t
            if jj < CONV_A - 1:
                nsa_o[j] = sa_ref[jj]
            else:
                nsa_o[j] = cv_ref[pl.ds((jj - (CONV_A - 1)) * n_batch, n_batch), :]


def _convs(u, cv, ba, state_b_tm, state_a_tm, conv_b_w, conv_b_bias, conv_a_w, *, tc, cb, n_seq, seq, n_batch):
    t = u.shape[0]
    n_prompt = n_seq * seq
    n_prompt_tiles = n_prompt // tc
    tiles_per_seq = seq // tc
    wrap_b, wrap_a = 256, 16
    assert t - n_prompt == tc and tc % n_batch == 0 and seq % tc == 0
    assert wrap_b >= HIST_B and wrap_a >= HIST_A and tc % wrap_b == 0 and tc % CONV_CHUNK == 0

    def seq_of(i):
        return jnp.minimum(i // tiles_per_seq, n_seq - 1)

    tile = pl.BlockSpec((tc, cb), lambda c, i: (i, c))
    kern = functools.partial(_conv_kernel, tc=tc, n_prompt_tiles=n_prompt_tiles, tiles_per_seq=tiles_per_seq,
                             n_batch=n_batch)
    return pl.pallas_call(
        kern,
        grid=(D // cb, t // tc),
        in_specs=[tile, tile, tile,
                  pl.BlockSpec((wrap_b, cb), lambda c, i: ((seq_of(i) + 1) * (seq // wrap_b) - 1, c)),
                  pl.BlockSpec((wrap_a, cb), lambda c, i: ((seq_of(i) + 1) * (seq // wrap_a) - 1, c)),
                  pl.BlockSpec((CONV_B - 1, n_batch, cb), lambda c, i: (0, 0, c)),
                  pl.BlockSpec((CONV_A - 1, n_batch, cb), lambda c, i: (0, 0, c)),
                  pl.BlockSpec((CONV_B, cb), lambda c, i: (0, c)),
                  pl.BlockSpec((CONV_B, SUBLANES, cb), lambda c, i: (0, 0, c)),
                  pl.BlockSpec((1, cb), lambda c, i: (0, c)),
                  pl.BlockSpec((CONV_A, cb), lambda c, i: (0, c))],
        out_specs=[tile, tile,
                   pl.BlockSpec((None, CONV_B - 1, cb), lambda c, i: (seq_of(i), 0, c)),
                   pl.BlockSpec((None, CONV_A - 1, cb), lambda c, i: (seq_of(i), 0, c)),
                   pl.BlockSpec((CONV_B - 1, n_batch, cb), lambda c, i: (0, 0, c)),
                   pl.BlockSpec((CONV_A - 1, n_batch, cb), lambda c, i: (0, 0, c))],
        out_shape=[jax.ShapeDtypeStruct((t, D), F32), jax.ShapeDtypeStruct((t, D), BF16),
                   jax.ShapeDtypeStruct((n_seq, CONV_B - 1, D), F32),
                   jax.ShapeDtypeStruct((n_seq, CONV_A - 1, D), F32),
                   jax.ShapeDtypeStruct((CONV_B - 1, n_batch, D), F32),
                   jax.ShapeDtypeStruct((CONV_A - 1, n_batch, D), F32)],
        scratch_shapes=[pltpu.VMEM((HIST_B + tc, cb), F32), pltpu.VMEM((HIST_A + tc, cb), F32)],
        compiler_params=_params("arbitrary", "arbitrary"),
        name="convs",
    )(u, cv, ba, u, cv, state_b_tm, state_a_tm, conv_b_w,
      jnp.broadcast_to(conv_b_w[:, None, :], (CONV_B, SUBLANES, D)), conv_b_bias, conv_a_w)


def _mix_out_kernel(cb_ref, lg_ref, lb_ref, ya_ref, ga_ref, gb_ref, woa_ref, wob_ref, wo_ref, x_ref, o_ref, z_scr):
    c = cb_ref[...]
    mu = jnp.mean(c, axis=-1, keepdims=True)
    xc = c - mu
    var = jnp.mean(xc * xc, axis=-1, keepdims=True)
    y = xc * lax.rsqrt(var + LN_EPS) * lg_ref[...] + lb_ref[...]
    z_scr[...] = (y * jax.nn.sigmoid(y)).astype(BF16)

    y_a = jnp.dot(ya_ref[...], woa_ref[...], preferred_element_type=F32)
    y_b = jnp.dot(z_scr[...], wob_ref[...], preferred_element_type=F32)
    merged = (ga_ref[...].astype(F32) * y_a + gb_ref[...].astype(F32) * y_b).astype(BF16)
    o_ref[...] = x_ref[...] + jnp.dot(merged, wo_ref[...], preferred_element_type=F32)


def _mix_out(conv_b, ln_g, ln_b, ya_in, ga, gb, w_out_a, w_out_b, w_o, layer, x, *, tm):
    t = conv_b.shape[0]
    row = pl.BlockSpec((tm, D), lambda m: (m, 0))
    vec = pl.BlockSpec((1, D), lambda m: (0, 0))
    wfull = pl.BlockSpec((None, D, D), lambda m: (layer, 0, 0), pipeline_mode=pl.Buffered(1))
    return pl.pallas_call(
        _mix_out_kernel,
        grid=(t // tm,),
        in_specs=[row, vec, vec, row, row, row, wfull, wfull, wfull, row],
        out_specs=row,
        out_shape=jax.ShapeDtypeStruct((t, D), F32),
        scratch_shapes=[pltpu.VMEM((tm, D), BF16)],
        compiler_params=_params("arbitrary"),
        name="mix_out",
    )(conv_b, ln_g, ln_b, ya_in, ga, gb, w_out_a, w_out_b, w_o, x)


def _ffn_kernel(x_ref, g_ref, wg_ref, wu_ref, wd_ref, gf_ref, o_ref, h_scr, *, final_norm):
    j = pl.program_id(1)

    @pl.when(j == 0)
    def _():
        x = x_ref[...]
        h_scr[...] = _rms(x, g_ref[...]).astype(BF16)
        o_ref[...] = x

    h = h_scr[...]
    gate = jnp.dot(h, wg_ref[...].astype(BF16), preferred_element_type=F32)
    up = jnp.dot(h, wu_ref[...].astype(BF16), preferred_element_type=F32)
    act = (gate * jax.nn.sigmoid(gate) * up).astype(BF16)
    o_ref[...] += jnp.dot(act, wd_ref[...].astype(BF16), preferred_element_type=F32)

    if final_norm:
        @pl.when(j == pl.num_programs(1) - 1)
        def _():
            o_ref[...] = _rms(o_ref[...], gf_ref[...])


def _ffn(x, g, w_gate, w_up, w_down, layer, g_final, *, tm, tf, final_norm):
    t = x.shape[0]
    row = pl.BlockSpec((tm, D), lambda m, j: (m, 0))
    vec = pl.BlockSpec((1, D), lambda m, j: (0, 0))
    return pl.pallas_call(
        functools.partial(_ffn_kernel, final_norm=final_norm),
        grid=(t // tm, D_FF // tf),
        in_specs=[pl.BlockSpec((tm, D), lambda m, j: (m, 0), pipeline_mode=pl.Buffered(1)), vec,
                  pl.BlockSpec((None, D, tf), lambda m, j: (layer, 0, j)),
                  pl.BlockSpec((None, D, tf), lambda m, j: (layer, 0, j)),
                  pl.BlockSpec((None, tf, D), lambda m, j: (layer, j, 0)),
                  vec],
        out_specs=row,
        out_shape=jax.ShapeDtypeStruct((t, D), F32),
        scratch_shapes=[pltpu.VMEM((tm, D), BF16)],
        compiler_params=_params("arbitrary", "arbitrary"),
        name="ffn",
    )(x, g, w_gate, w_up, w_down, g_final)


def kernel(x_prompt, x_sample, state_conv_a, state_conv_b, norm_mix_g, w_in, conv_a_w, w_out_a, conv_b_w,
           conv_b_bias, ln_b_g, ln_b_b, w_out_b, w_o, norm_ffn_g, w_gate, w_up, w_down, final_norm_g):
    n_seq, seq, _ = x_prompt.shape
    n_batch, n_dec, _ = x_sample.shape
    depth = w_in.shape[0]
    n_prompt = n_seq * seq
    n_sample = n_batch * n_dec
    seg = seq // SUBLANES

    x = jnp.concatenate(
        [x_prompt.reshape(n_seq, SUBLANES, seg, D).transpose(0, 2, 1, 3).reshape(n_prompt, D),
         x_sample.transpose(1, 0, 2).reshape(n_sample, D)], axis=0)
    w_out_a, w_out_b, w_o = (w.astype(BF16) for w in (w_out_a, w_out_b, w_o))
    g_final = final_norm_g.reshape(1, D)
    pa, pb, sa, sb = [], [], [], []
    for l in range(depth):
        vec = lambda v: v[l].reshape(1, D)
        ba, cv, u, ga, gb = _in_proj(x, vec(norm_mix_g), w_in, l, tm=1024, tn=256)
        conv_b, ya_in, new_pb, new_pa, new_sb, new_sa = _convs(
            u, cv, ba, state_conv_b[l].transpose(1, 0, 2), state_conv_a[l].transpose(1, 0, 2),
            conv_b_w[l], vec(conv_b_bias), conv_a_w[l],
            tc=n_sample, cb=256, n_seq=n_seq, seq=seq, n_batch=n_batch)
        x = _mix_out(conv_b, vec(ln_b_g), vec(ln_b_b), ya_in, ga, gb, w_out_a, w_out_b, w_o, l, x, tm=256)
        x = _ffn(x, vec(norm_ffn_g), w_gate, w_up, w_down, l, g_final,
                 tm=1024, tf=256, final_norm=(l == depth - 1))
        pa.append(new_pa)
        pb.append(new_pb)
        sa.append(new_sa.transpose(1, 0, 2))
        sb.append(new_sb.transpose(1, 0, 2))

    y_prompt = x[:n_prompt].reshape(n_seq, seg, SUBLANES, D).transpose(0, 2, 1, 3).reshape(n_seq, seq, D)
    y_sample = x[n_prompt:].reshape(n_dec, n_batch, D).transpose(1, 0, 2)
    return (y_prompt, y_sample, jnp.stack(pa), jnp.stack(pb), jnp.stack(sa), jnp.stack(sb))
```

```python
import functools

import jax
import jax.numpy as jnp
from jax import lax
from jax.experimental import pallas as pl
from jax.experimental.pallas import tpu as pltpu

D = 2048
D_FF = 5632
CONV_A = 3
CONV_B = 31
RMS_EPS = 1e-6
LN_EPS = 1e-5
SUBLANES = 8
LANES = 128
MXU_SLICES = 8
HIST_B = (CONV_B - 1) * SUBLANES
HIST_A = (CONV_A - 1) * SUBLANES
CONV_CHUNK = 32
WRAP_B = 256
WRAP_A = 16
VMEM_LIMIT = 56 * 1024 * 1024

BF16 = jnp.bfloat16
F32 = jnp.float32


def _params(*sem):
    return pltpu.CompilerParams(dimension_semantics=sem, vmem_limit_bytes=VMEM_LIMIT)


def _exact_zero(v):
    both = jnp.concatenate([v[:, :LANES], v[:, LANES:2 * LANES]], axis=0)
    bits = lax.bitcast_convert_type(both, jnp.int32)
    zero = lax.shift_right_logical(lax.shift_right_logical(bits, 16), 16)
    return zero.astype(F32).astype(BF16)


def _rms(x, g):
    return x * lax.rsqrt(jnp.mean(x * x, axis=-1, keepdims=True) + RMS_EPS) * g


def _prev_segment(rows):
    moved = pltpu.roll(rows, shift=1, axis=0)
    sub = lax.broadcasted_iota(jnp.int32, rows.shape, 0)
    return jnp.where(sub == 0, 0.0, moved)


def _in_proj_b_kernel(x_ref, g_ref, wga, wgb, wta, wtb, u_o, ga_o, gb_o, h_scr):
    @pl.when(pl.program_id(1) == 0)
    def _():
        h_scr[...] = _rms(x_ref[...], g_ref[...]).astype(BF16)

    h = h_scr[...]

    def dot(w):
        return jnp.dot(h, w[...].astype(BF16), preferred_element_type=F32)

    u_o[...] = dot(wga) * jax.nn.sigmoid(dot(wgb))
    ga_o[...] = jax.nn.sigmoid(dot(wta)).astype(BF16)
    gb_o[...] = jax.nn.sigmoid(dot(wtb)).astype(BF16)


def _in_proj_b(x, g, w_in, layer, *, tm, tn):
    t = x.shape[0]
    nb = D // tn
    w_specs = [pl.BlockSpec((None, D, tn), functools.partial(lambda m, n, grp: (layer, 0, grp * nb + n), grp=grp))
               for grp in (3, 4, 5, 6)]
    tile = pl.BlockSpec((tm, tn), lambda m, n: (m, n))
    return pl.pallas_call(
        _in_proj_b_kernel,
        grid=(t // tm, nb),
        in_specs=[pl.BlockSpec((tm, D), lambda m, n: (m, 0), pipeline_mode=pl.Buffered(1)),
                  pl.BlockSpec((1, D), lambda m, n: (0, 0))] + w_specs,
        out_specs=[tile] * 3,
        out_shape=[jax.ShapeDtypeStruct((t, D), F32), jax.ShapeDtypeStruct((t, D), BF16),
                   jax.ShapeDtypeStruct((t, D), BF16)],
        scratch_shapes=[pltpu.VMEM((tm, D), BF16)],
        compiler_params=_params("arbitrary", "arbitrary"),
        name="in_proj_b",
    )(x, g, w_in, w_in, w_in, w_in)


def _in_proj_a_kernel(x_ref, g_ref, wb, wc, wv, u_ref, hist_ref, sb_ref, wb8_ref, bias_ref,
                      ba_o, cv_o, cb_o, pb_o, nsb_o, h_scr, xh_scr,
                      *, tm, n_prompt_tiles, tiles_per_seq, n_batch):
    m = pl.program_id(0)
    groups = CONV_CHUNK // SUBLANES

    @pl.when(pl.program_id(1) == 0)
    def _():
        h_scr[...] = _rms(x_ref[...], g_ref[...]).astype(BF16)

    n_chunks = tm // CONV_CHUNK

    def interleaved(conv_chunk):
        per_slice = n_chunks // MXU_SLICES
        slice_rows = tm // MXU_SLICES
        for s in range(MXU_SLICES):
            for ci in range(s * per_slice, (s + 1) * per_slice):
                conv_chunk(ci)
            rows = pl.ds(s * slice_rows, slice_rows)
            h = h_scr[rows, :]
            ba_o[rows, :] = jnp.dot(h, wb[...], preferred_element_type=F32).astype(BF16)
            cv_o[rows, :] = (jnp.dot(h, wc[...], preferred_element_type=F32)
                             * jnp.dot(h, wv[...], preferred_element_type=F32))

    @pl.when(m < n_prompt_tiles)
    def _prompt():
        first = m % tiles_per_seq == 0
        w0 = hist_ref.shape[0] - HIST_B
        for j in range(CONV_B - 1):
            rows = hist_ref[pl.ds(w0 + j * SUBLANES, SUBLANES), :]
            xh_scr[pl.ds(j * SUBLANES, SUBLANES), :] = jnp.where(first, _prev_segment(rows), rows)
        bias8 = jnp.broadcast_to(bias_ref[...], (SUBLANES, bias_ref.shape[1]))

        def conv_chunk(ci):
            c0 = ci * CONV_CHUNK
            accs = [bias8] * groups
            for k in range(CONV_B):
                w = wb8_ref[k]
                for gi in range(groups):
                    r = c0 + (gi + k) * SUBLANES - HIST_B
                    if r < 0:
                        src = xh_scr[pl.ds(r + HIST_B, SUBLANES), :]
                    else:
                        src = u_ref[pl.ds(r, SUBLANES), :]
                    accs[gi] = accs[gi] + src * w
            for gi in range(groups):
                cb_o[pl.ds(c0 + gi * SUBLANES, SUBLANES), :] = accs[gi]
            return accs[-1]

        interleaved(conv_chunk)
        for j in range(CONV_B - 1):
            pb_o[pl.ds(j, 1), :] = u_ref[pl.ds(tm - HIST_B + j * SUBLANES + SUBLANES - 1, 1), :]

    @pl.when(m == n_prompt_tiles)
    def _sample():
        n_t = tm // n_batch
        bias8 = jnp.broadcast_to(bias_ref[...], (SUBLANES, bias_ref.shape[1]))

        def conv_chunk(ci):
            t, r0 = divmod(ci * CONV_CHUNK, n_batch)
            accs = [bias8] * groups
            for k in range(CONV_B):
                j = t + k
                w = wb8_ref[k]
                for gi in range(groups):
                    g0 = r0 + gi * SUBLANES
                    if j < CONV_B - 1:
                        src = sb_ref[j, pl.ds(g0, SUBLANES), :]
                    else:
                        src = u_ref[pl.ds((j - (CONV_B - 1)) * n_batch + g0, SUBLANES), :]
                    accs[gi] = accs[gi] + src * w
            for gi in range(groups):
                cb_o[pl.ds(t * n_batch + r0 + gi * SUBLANES, SUBLANES), :] = accs[gi]
            return accs[-1]

        interleaved(conv_chunk)
        for j in range(CONV_B - 1):
            jj = j + n_t
            if jj < CONV_B - 1:
                nsb_o[j] = sb_ref[jj]
            else:
                nsb_o[j] = u_ref[pl.ds((jj - (CONV_B - 1)) * n_batch, n_batch), :]


def _in_proj_a(x, g, w_a, layer, u, state_b_tm, conv_b_w, conv_b_bias, *, tm, tn, n_seq, seq, n_batch):
    t = x.shape[0]
    nb = D // tn
    n_prompt_tiles = (n_seq * seq) // tm
    tiles_per_seq = seq // tm
    assert t - n_seq * seq == tm and tm % n_batch == 0 and seq % tm == 0 and tm % WRAP_B == 0
    per_tile = tm // WRAP_B

    def hist_index(m, n):
        seq_last = (jnp.minimum(m // tiles_per_seq, n_seq - 1) + 1) * (seq // WRAP_B) - 1
        return jnp.where(m % tiles_per_seq == 0, seq_last, m * per_tile - 1), n

    def sample_only(m, n):
        return jnp.where(m == n_prompt_tiles, n, 0)

    def pb_index(m, n):
        return jnp.minimum(m // tiles_per_seq, n_seq - 1), 0, jnp.where(m == n_prompt_tiles, nb - 1, n)

    w_specs = [pl.BlockSpec((None, D, tn), functools.partial(lambda m, n, grp: (layer, 0, grp * nb + n), grp=grp))
               for grp in range(3)]
    tile = pl.BlockSpec((tm, tn), lambda m, n: (m, n))
    state = pl.BlockSpec((CONV_B - 1, n_batch, tn), lambda m, n: (0, 0, sample_only(m, n)))
    kern = functools.partial(_in_proj_a_kernel, tm=tm, n_prompt_tiles=n_prompt_tiles,
                             tiles_per_seq=tiles_per_seq, n_batch=n_batch)
    return pl.pallas_call(
        kern,
        grid=(t // tm, nb),
        in_specs=[pl.BlockSpec((tm, D), lambda m, n: (m, 0), pipeline_mode=pl.Buffered(1)),
                  pl.BlockSpec((1, D), lambda m, n: (0, 0))] + w_specs + [
                  tile,
                  pl.BlockSpec((WRAP_B, tn), hist_index),
                  state,
                  pl.BlockSpec((CONV_B, SUBLANES, tn), lambda m, n: (0, 0, n)),
                  pl.BlockSpec((1, tn), lambda m, n: (0, n))],
        out_specs=[tile, tile, tile,
                   pl.BlockSpec((None, CONV_B - 1, tn), pb_index),
                   state],
        out_shape=[jax.ShapeDtypeStruct((t, D), BF16), jax.ShapeDtypeStruct((t, D), F32),
                   jax.ShapeDtypeStruct((t, D), F32),
                   jax.ShapeDtypeStruct((n_seq, CONV_B - 1, D), F32),
                   jax.ShapeDtypeStruct((CONV_B - 1, n_batch, D), F32)],
        scratch_shapes=[pltpu.VMEM((tm, D), BF16), pltpu.VMEM((HIST_B, tn), F32)],
        compiler_params=_params("arbitrary", "arbitrary"),
        name="in_proj_a",
    )(x, g, w_a, w_a, w_a, u, u, state_b_tm,
      jnp.broadcast_to(conv_b_w[:, None, :], (CONV_B, SUBLANES, D)), conv_b_bias)


def _conv_a_kernel(cv_ref, ba_ref, cw_ref, sa_ref, wa_ref, ya_o, pa_o, nsa_o, xc_scr,
                   *, tc, n_prompt_tiles, tiles_per_seq, n_batch):
    i = pl.program_id(1)

    def taps(src_of_tap, rows):
        acc = None
        for k in range(CONV_A):
            term = src_of_tap(k) * wa_ref[k:k + 1, :]
            acc = term if acc is None else acc + term
        ya_o[rows, :] = (ba_ref[rows, :].astype(F32) * acc).astype(BF16)

    @pl.when(i < n_prompt_tiles)
    def _prompt():
        @pl.when(i % tiles_per_seq == 0)
        def _():
            w0 = cw_ref.shape[0] - HIST_A
            for j in range(CONV_A - 1):
                xc_scr[pl.ds(j * SUBLANES, SUBLANES), :] = _prev_segment(
                    cw_ref[pl.ds(w0 + j * SUBLANES, SUBLANES), :])

        @pl.when(i % tiles_per_seq != 0)
        def _():
            xc_scr[0:HIST_A, :] = xc_scr[tc:tc + HIST_A, :]

        xc_scr[HIST_A:HIST_A + tc, :] = cv_ref[...]

        def chunk(ci, carry):
            r0 = pl.multiple_of(ci * CONV_CHUNK, CONV_CHUNK)
            taps(lambda k: xc_scr[pl.ds(r0 + k * SUBLANES, CONV_CHUNK), :], pl.ds(r0, CONV_CHUNK))
            return carry

        lax.fori_loop(0, tc // CONV_CHUNK, chunk, 0)

        @pl.when(i % tiles_per_seq == tiles_per_seq - 1)
        def _():
            for j in range(CONV_A - 1):
                pa_o[pl.ds(j, 1), :] = cv_ref[pl.ds(tc - HIST_A + j * SUBLANES + SUBLANES - 1, 1), :]

    @pl.when(i == n_prompt_tiles)
    def _sample():
        n_t = tc // n_batch
        for t in range(n_t):
            def src_of_tap(k, t=t):
                j = t + k
                if j < CONV_A - 1:
                    return sa_ref[j]
                return cv_ref[pl.ds((j - (CONV_A - 1)) * n_batch, n_batch), :]

            taps(src_of_tap, pl.ds(t * n_batch, n_batch))
        for j in range(CONV_A - 1):
            jj = j + n_t
            if jj < CONV_A - 1:
                nsa_o[j] = sa_ref[jj]
            else:
                nsa_o[j] = cv_ref[pl.ds((jj - (CONV_A - 1)) * n_batch, n_batch), :]


def _conv_a(cv, ba, state_a_tm, conv_a_w, *, tc, cb, n_seq, seq, n_batch):
    t = cv.shape[0]
    n_prompt_tiles = (n_seq * seq) // tc
    tiles_per_seq = seq // tc
    assert t - n_seq * seq == tc and tc % n_batch == 0 and seq % tc == 0 and tc % CONV_CHUNK == 0

    def seq_of(i):
        return jnp.minimum(i // tiles_per_seq, n_seq - 1)

    tile = pl.BlockSpec((tc, cb), lambda c, i: (i, c))
    state = pl.BlockSpec((CONV_A - 1, n_batch, cb), lambda c, i: (0, 0, c))
    kern = functools.partial(_conv_a_kernel, tc=tc, n_prompt_tiles=n_prompt_tiles, tiles_per_seq=tiles_per_seq,
                             n_batch=n_batch)
    return pl.pallas_call(
        kern,
        grid=(D // cb, t // tc),
        in_specs=[tile, tile,
                  pl.BlockSpec((WRAP_A, cb), lambda c, i: ((seq_of(i) + 1) * (seq // WRAP_A) - 1, c)),
                  state,
                  pl.BlockSpec((CONV_A, cb), lambda c, i: (0, c))],
        out_specs=[tile,
                   pl.BlockSpec((None, CONV_A - 1, cb), lambda c, i: (seq_of(i), 0, c)),
                   state],
        out_shape=[jax.ShapeDtypeStruct((t, D), BF16),
                   jax.ShapeDtypeStruct((n_seq, CONV_A - 1, D), F32),
                   jax.ShapeDtypeStruct((CONV_A - 1, n_batch, D), F32)],
        scratch_shapes=[pltpu.VMEM((HIST_A + tc, cb), F32)],
        compiler_params=_params("arbitrary", "arbitrary"),
        name="conv_a",
    )(cv, ba, cv, state_a_tm, conv_a_w)


def _mix_out_kernel(cb_ref, lg_ref, lb_ref, ya_ref, ga_ref, gb_ref, woa_ref, wob_ref, wo_ref, x_ref, o_ref, z_scr):
    c = cb_ref[...]
    mu = jnp.mean(c, axis=-1, keepdims=True)
    xc = c - mu
    var = jnp.mean(xc * xc, axis=-1, keepdims=True)
    y = xc * lax.rsqrt(var + LN_EPS) * lg_ref[...] + lb_ref[...]
    z_scr[...] = (y * jax.nn.sigmoid(y)).astype(BF16)

    y_a = jnp.dot(ya_ref[...], woa_ref[...], preferred_element_type=F32)
    y_b = jnp.dot(z_scr[...], wob_ref[...], preferred_element_type=F32)
    merged = (ga_ref[...].astype(F32) * y_a + gb_ref[...].astype(F32) * y_b).astype(BF16)
    o_ref[...] = x_ref[...] + jnp.dot(merged, wo_ref[...], preferred_element_type=F32)


def _mix_out(conv_b, ln_g, ln_b, ya_in, ga, gb, w_out_a, w_out_b, w_o, layer, x, *, tm):
    t = conv_b.shape[0]
    row = pl.BlockSpec((tm, D), lambda m: (m, 0))
    vec = pl.BlockSpec((1, D), lambda m: (0, 0))
    wfull = pl.BlockSpec((None, D, D), lambda m: (layer, 0, 0), pipeline_mode=pl.Buffered(1))
    return pl.pallas_call(
        _mix_out_kernel,
        grid=(t // tm,),
        in_specs=[row, vec, vec, row, row, row, wfull, wfull, wfull, row],
        out_specs=row,
        out_shape=jax.ShapeDtypeStruct((t, D), F32),
        scratch_shapes=[pltpu.VMEM((tm, D), BF16)],
        compiler_params=_params("arbitrary"),
        name="mix_out",
    )(conv_b, ln_g, ln_b, ya_in, ga, gb, w_out_a, w_out_b, w_o, x)


def _ffn_kernel(x_ref, g_ref, wg_ref, wu_ref, wd_ref, gf_ref, o_ref, h_scr, *, final_norm):
    j = pl.program_id(1)

    @pl.when(j == 0)
    def _():
        x = x_ref[...]
        h_scr[...] = _rms(x, g_ref[...]).astype(BF16)
        o_ref[...] = x

    h = h_scr[...]
    gate = jnp.dot(h, wg_ref[...].astype(BF16), preferred_element_type=F32)
    up = jnp.dot(h, wu_ref[...].astype(BF16), preferred_element_type=F32)
    act = (gate * jax.nn.sigmoid(gate) * up).astype(BF16)
    o_ref[...] += jnp.dot(act, wd_ref[...].astype(BF16), preferred_element_type=F32)

    if final_norm:
        @pl.when(j == pl.num_programs(1) - 1)
        def _():
            o_ref[...] = _rms(o_ref[...], gf_ref[...])


def _ffn(x, g, w_gate, w_up, w_down, layer, g_final, *, tm, tf, final_norm):
    t = x.shape[0]
    row = pl.BlockSpec((tm, D), lambda m, j: (m, 0))
    vec = pl.BlockSpec((1, D), lambda m, j: (0, 0))
    return pl.pallas_call(
        functools.partial(_ffn_kernel, final_norm=final_norm),
        grid=(t // tm, D_FF // tf),
        in_specs=[pl.BlockSpec((tm, D), lambda m, j: (m, 0), pipeline_mode=pl.Buffered(1)), vec,
                  pl.BlockSpec((None, D, tf), lambda m, j: (layer, 0, j)),
                  pl.BlockSpec((None, D, tf), lambda m, j: (layer, 0, j)),
                  pl.BlockSpec((None, tf, D), lambda m, j: (layer, j, 0)),
                  vec],
        out_specs=row,
        out_shape=jax.ShapeDtypeStruct((t, D), F32),
        scratch_shapes=[pltpu.VMEM((tm, D), BF16)],
        compiler_params=_params("arbitrary", "arbitrary"),
        name="ffn",
    )(x, g, w_gate, w_up, w_down, g_final)


def kernel(x_prompt, x_sample, state_conv_a, state_conv_b, norm_mix_g, w_in, conv_a_w, w_out_a, conv_b_w,
           conv_b_bias, ln_b_g, ln_b_b, w_out_b, w_o, norm_ffn_g, w_gate, w_up, w_down, final_norm_g):
    n_seq, seq, _ = x_prompt.shape
    n_batch, n_dec, _ = x_sample.shape
    depth = w_in.shape[0]
    n_prompt = n_seq * seq
    n_sample = n_batch * n_dec
    seg = seq // SUBLANES
    shape = dict(n_seq=n_seq, seq=seq, n_batch=n_batch)

    x = jnp.concatenate(
        [x_prompt.reshape(n_seq, SUBLANES, seg, D).transpose(0, 2, 1, 3).reshape(n_prompt, D),
         x_sample.transpose(1, 0, 2).reshape(n_sample, D)], axis=0)
    w_a = w_in[:, :, :3 * D].astype(BF16)
    w_out_a, w_out_b, w_o = (w.astype(BF16) for w in (w_out_a, w_out_b, w_o))
    g_final = final_norm_g.reshape(1, D)
    pa, pb, sa, sb = [], [], [], []
    for l in range(depth):
        vec = lambda v: v[l].reshape(1, D)
        u, ga, gb = _in_proj_b(x, vec(norm_mix_g), w_in, l, tm=n_sample, tn=256)
        ba, cv, conv_b, new_pb, new_sb = _in_proj_a(
            x, vec(norm_mix_g), w_a, l, u, state_conv_b[l].transpose(1, 0, 2), conv_b_w[l], vec(conv_b_bias),
            tm=n_sample, tn=256, **shape)
        ya_in, new_pa, new_sa = _conv_a(cv, ba, state_conv_a[l].transpose(1, 0, 2), conv_a_w[l],
                                        tc=n_sample, cb=512, **shape)
        x = _mix_out(conv_b, vec(ln_b_g), vec(ln_b_b), ya_in, ga, gb, w_out_a, w_out_b, w_o, l, x, tm=256)
        x = _ffn(x, vec(norm_ffn_g), w_gate, w_up, w_down, l, g_final,
                 tm=1024, tf=256, final_norm=(l == depth - 1))
        pa.append(new_pa)
        pb.append(new_pb)
        sa.append(new_sa.transpose(1, 0, 2))
        sb.append(new_sb.transpose(1, 0, 2))

    y_prompt = x[:n_prompt].reshape(n_seq, seg, SUBLANES, D).transpose(0, 2, 1, 3).reshape(n_seq, seq, D)
    y_sample = x[n_prompt:].reshape(n_dec, n_batch, D).transpose(1, 0, 2)
    return (y_prompt, y_sample, jnp.stack(pa), jnp.stack(pb), jnp.stack(sa), jnp.stack(sb))
```

```python
import functools

import jax
import jax.numpy as jnp
from jax import lax
from jax.experimental import pallas as pl
from jax.experimental.pallas import tpu as pltpu

D = 2048
D_FF = 5632
CONV_A = 3
CONV_B = 31
RMS_EPS = 1e-6
LN_EPS = 1e-5
SUBLANES = 8
MXU_SLICE_ROWS = 512
HIST_B = (CONV_B - 1) * SUBLANES
HIST_A = (CONV_A - 1) * SUBLANES
CONV_CHUNK = 32
WRAP_B = 256
WRAP_A = 16
VMEM_LIMIT = 58 * 1024 * 1024

BF16 = jnp.bfloat16
F32 = jnp.float32


def _params(*sem):
    return pltpu.CompilerParams(dimension_semantics=sem, vmem_limit_bytes=VMEM_LIMIT)


def _rms(x, g):
    return x * lax.rsqrt(jnp.mean(x * x, axis=-1, keepdims=True) + RMS_EPS) * g


def _prev_segment(rows):
    moved = pltpu.roll(rows, shift=1, axis=0)
    sub = lax.broadcasted_iota(jnp.int32, rows.shape, 0)
    return jnp.where(sub == 0, 0.0, moved)


def _in_proj_b_kernel(x_ref, g_ref, wga, wgb, wta, wtb, u_o, ga_o, gb_o, h_scr):
    @pl.when(pl.program_id(1) == 0)
    def _():
        h_scr[...] = _rms(x_ref[...], g_ref[...]).astype(BF16)

    h = h_scr[...]

    def dot(w):
        return jnp.dot(h, w[...].astype(BF16), preferred_element_type=F32)

    u_o[...] = dot(wga) * jax.nn.sigmoid(dot(wgb))
    ga_o[...] = jax.nn.sigmoid(dot(wta)).astype(BF16)
    gb_o[...] = jax.nn.sigmoid(dot(wtb)).astype(BF16)


def _in_proj_b(x, g, w_in, layer, *, tm, tn):
    t = x.shape[0]
    nb = D // tn
    w_specs = [pl.BlockSpec((None, D, tn), functools.partial(lambda m, n, grp: (layer, 0, grp * nb + n), grp=grp))
               for grp in (3, 4, 5, 6)]
    tile = pl.BlockSpec((tm, tn), lambda m, n: (m, n))
    return pl.pallas_call(
        _in_proj_b_kernel,
        grid=(t // tm, nb),
        in_specs=[pl.BlockSpec((tm, D), lambda m, n: (m, 0), pipeline_mode=pl.Buffered(1)),
                  pl.BlockSpec((1, D), lambda m, n: (0, 0))] + w_specs,
        out_specs=[tile] * 3,
        out_shape=[jax.ShapeDtypeStruct((t, D), F32), jax.ShapeDtypeStruct((t, D), BF16),
                   jax.ShapeDtypeStruct((t, D), BF16)],
        scratch_shapes=[pltpu.VMEM((tm, D), BF16)],
        compiler_params=_params("arbitrary", "arbitrary"),
        name="in_proj_b",
    )(x, g, w_in, w_in, w_in, w_in)


def _in_proj_a_kernel(x_ref, g_ref, wb, wc, wv, u_ref, hist_ref, sb_ref, wb8_ref, bias_ref,
                      ba_o, cv_o, cb_o, nsb_o, h_scr, xh_scr,
                      *, tm, n_prompt_tiles, tiles_per_seq, n_batch):
    m = pl.program_id(0)
    groups = CONV_CHUNK // SUBLANES

    @pl.when(pl.program_id(1) == 0)
    def _():
        h_scr[...] = _rms(x_ref[...], g_ref[...]).astype(BF16)

    n_chunks = tm // CONV_CHUNK

    def interleaved(conv_chunk):
        n_slices = tm // MXU_SLICE_ROWS
        pieces = [(s, w) for s in range(n_slices) for w in (wb, wc, wv)]
        bounds = [round(p * n_chunks / len(pieces)) for p in range(len(pieces) + 1)]
        for p, (s, w) in enumerate(pieces):
            for ci in range(bounds[p], bounds[p + 1]):
                conv_chunk(ci)
            rows = pl.ds(s * MXU_SLICE_ROWS, MXU_SLICE_ROWS)
            y = jnp.dot(h_scr[rows, :], w[...], preferred_element_type=F32)
            if w is wb:
                ba_o[rows, :] = y.astype(BF16)
            elif w is wc:
                cv_o[rows, :] = y
            else:
                cv_o[rows, :] = cv_o[rows, :] * y

    @pl.when(m < n_prompt_tiles)
    def _prompt():
        first = m % tiles_per_seq == 0
        w0 = hist_ref.shape[0] - HIST_B
        for j in range(CONV_B - 1):
            rows = hist_ref[pl.ds(w0 + j * SUBLANES, SUBLANES), :]
            xh_scr[pl.ds(j * SUBLANES, SUBLANES), :] = jnp.where(first, _prev_segment(rows), rows)
        bias8 = jnp.broadcast_to(bias_ref[...], (SUBLANES, bias_ref.shape[1]))

        def conv_chunk(ci):
            c0 = ci * CONV_CHUNK
            accs = [bias8] * groups
            for k in range(CONV_B):
                w = wb8_ref[k]
                for gi in range(groups):
                    r = c0 + (gi + k) * SUBLANES - HIST_B
                    if r < 0:
                        src = xh_scr[pl.ds(r + HIST_B, SUBLANES), :]
                    else:
                        src = u_ref[pl.ds(r, SUBLANES), :]
                    accs[gi] = accs[gi] + src * w
            for gi in range(groups):
                cb_o[pl.ds(c0 + gi * SUBLANES, SUBLANES), :] = accs[gi]
            return accs[-1]

        interleaved(conv_chunk)

    @pl.when(m == n_prompt_tiles)
    def _sample():
        n_t = tm // n_batch
        bias8 = jnp.broadcast_to(bias_ref[...], (SUBLANES, bias_ref.shape[1]))

        def conv_chunk(ci):
            t, r0 = divmod(ci * CONV_CHUNK, n_batch)
            accs = [bias8] * groups
            for k in range(CONV_B):
                j = t + k
                w = wb8_ref[k]
                for gi in range(groups):
                    g0 = r0 + gi * SUBLANES
                    if j < CONV_B - 1:
                        src = sb_ref[j, pl.ds(g0, SUBLANES), :]
                    else:
                        src = u_ref[pl.ds((j - (CONV_B - 1)) * n_batch + g0, SUBLANES), :]
                    accs[gi] = accs[gi] + src * w
            for gi in range(groups):
                cb_o[pl.ds(t * n_batch + r0 + gi * SUBLANES, SUBLANES), :] = accs[gi]
            return accs[-1]

        interleaved(conv_chunk)
        for j in range(CONV_B - 1):
            jj = j + n_t
            if jj < CONV_B - 1:
                nsb_o[j] = sb_ref[jj]
            else:
                nsb_o[j] = u_ref[pl.ds((jj - (CONV_B - 1)) * n_batch, n_batch), :]


def _in_proj_a(x, g, w_a, layer, u, state_b_tm, conv_b_w, conv_b_bias, *, tm, tn, n_seq, seq, n_batch):
    t = x.shape[0]
    nb = D // tn
    n_prompt_tiles = (n_seq * seq) // tm
    tiles_per_seq = seq // tm
    assert t - n_seq * seq == tm and tm % n_batch == 0 and seq % tm == 0 and tm % WRAP_B == 0
    per_tile = tm // WRAP_B

    def hist_index(m, n):
        seq_last = (jnp.minimum(m // tiles_per_seq, n_seq - 1) + 1) * (seq // WRAP_B) - 1
        return jnp.where(m % tiles_per_seq == 0, seq_last, m * per_tile - 1), n

    def sample_only(m, n):
        return jnp.where(m == n_prompt_tiles, n, 0)

    w_specs = [pl.BlockSpec((None, D, tn), functools.partial(lambda m, n, grp: (layer, 0, grp * nb + n), grp=grp))
               for grp in range(3)]
    tile = pl.BlockSpec((tm, tn), lambda m, n: (m, n))
    state = pl.BlockSpec((CONV_B - 1, n_batch, tn), lambda m, n: (0, 0, sample_only(m, n)))
    kern = functools.partial(_in_proj_a_kernel, tm=tm, n_prompt_tiles=n_prompt_tiles,
                             tiles_per_seq=tiles_per_seq, n_batch=n_batch)
    return pl.pallas_call(
        kern,
        grid=(t // tm, nb),
        in_specs=[pl.BlockSpec((tm, D), lambda m, n: (m, 0), pipeline_mode=pl.Buffered(1)),
                  pl.BlockSpec((1, D), lambda m, n: (0, 0))] + w_specs + [
                  tile,
                  pl.BlockSpec((WRAP_B, tn), hist_index),
                  state,
                  pl.BlockSpec((CONV_B, SUBLANES, tn), lambda m, n: (0, 0, n)),
                  pl.BlockSpec((1, tn), lambda m, n: (0, n))],
        out_specs=[tile, tile, tile, state],
        out_shape=[jax.ShapeDtypeStruct((t, D), BF16), jax.ShapeDtypeStruct((t, D), F32),
                   jax.ShapeDtypeStruct((t, D), F32),
                   jax.ShapeDtypeStruct((CONV_B - 1, n_batch, D), F32)],
        scratch_shapes=[pltpu.VMEM((tm, D), BF16), pltpu.VMEM((HIST_B, tn), F32)],
        compiler_params=_params("arbitrary", "arbitrary"),
        name="in_proj_a",
    )(x, g, w_a, w_a, w_a, u, u, state_b_tm,
      jnp.broadcast_to(conv_b_w[:, None, :], (CONV_B, SUBLANES, D)), conv_b_bias)


def _conv_a_kernel(cv_ref, ba_ref, cw_ref, uw_ref, sa_ref, wa_ref, ya_o, pa_o, pb_o, nsa_o, xc_scr,
                   *, tc, n_prompt_tiles, tiles_per_seq, n_batch):
    i = pl.program_id(1)

    def taps(src_of_tap, rows):
        acc = None
        for k in range(CONV_A):
            term = src_of_tap(k) * wa_ref[k:k + 1, :]
            acc = term if acc is None else acc + term
        ya_o[rows, :] = (ba_ref[rows, :].astype(F32) * acc).astype(BF16)

    @pl.when(i < n_prompt_tiles)
    def _prompt():
        @pl.when(i % tiles_per_seq == 0)
        def _():
            w0 = cw_ref.shape[0] - HIST_A
            for j in range(CONV_A - 1):
                xc_scr[pl.ds(j * SUBLANES, SUBLANES), :] = _prev_segment(
                    cw_ref[pl.ds(w0 + j * SUBLANES, SUBLANES), :])

        @pl.when(i % tiles_per_seq != 0)
        def _():
            xc_scr[0:HIST_A, :] = xc_scr[tc:tc + HIST_A, :]

        xc_scr[HIST_A:HIST_A + tc, :] = cv_ref[...]

        def chunk(ci, carry):
            r0 = pl.multiple_of(ci * CONV_CHUNK, CONV_CHUNK)
            taps(lambda k: xc_scr[pl.ds(r0 + k * SUBLANES, CONV_CHUNK), :], pl.ds(r0, CONV_CHUNK))
            return carry

        lax.fori_loop(0, tc // CONV_CHUNK, chunk, 0)

        @pl.when(i % tiles_per_seq == tiles_per_seq - 1)
        def _():
            last = SUBLANES - 1
            for j in range(CONV_A - 1):
                pa_o[pl.ds(j, 1), :] = cv_ref[pl.ds(tc - HIST_A + j * SUBLANES + last, 1), :]
            for j in range(CONV_B - 1):
                pb_o[pl.ds(j, 1), :] = uw_ref[pl.ds(uw_ref.shape[0] - HIST_B + j * SUBLANES + last, 1), :]

    @pl.when(i == n_prompt_tiles)
    def _sample():
        n_t = tc // n_batch
        for t in range(n_t):
            def src_of_tap(k, t=t):
                j = t + k
                if j < CONV_A - 1:
                    return sa_ref[j]
                return cv_ref[pl.ds((j - (CONV_A - 1)) * n_batch, n_batch), :]

            taps(src_of_tap, pl.ds(t * n_batch, n_batch))
        for j in range(CONV_A - 1):
            jj = j + n_t
            if jj < CONV_A - 1:
                nsa_o[j] = sa_ref[jj]
            else:
                nsa_o[j] = cv_ref[pl.ds((jj - (CONV_A - 1)) * n_batch, n_batch), :]


def _conv_a(cv, ba, u, state_a_tm, conv_a_w, *, tc, cb, n_seq, seq, n_batch):
    t = cv.shape[0]
    n_prompt_tiles = (n_seq * seq) // tc
    tiles_per_seq = seq // tc
    assert t - n_seq * seq == tc and tc % n_batch == 0 and seq % tc == 0 and tc % CONV_CHUNK == 0

    def seq_of(i):
        return jnp.minimum(i // tiles_per_seq, n_seq - 1)

    tile = pl.BlockSpec((tc, cb), lambda c, i: (i, c))
    state = pl.BlockSpec((CONV_A - 1, n_batch, cb), lambda c, i: (0, 0, c))
    kern = functools.partial(_conv_a_kernel, tc=tc, n_prompt_tiles=n_prompt_tiles, tiles_per_seq=tiles_per_seq,
                             n_batch=n_batch)
    return pl.pallas_call(
        kern,
        grid=(D // cb, t // tc),
        in_specs=[tile, tile,
                  pl.BlockSpec((WRAP_A, cb), lambda c, i: ((seq_of(i) + 1) * (seq // WRAP_A) - 1, c)),
                  pl.BlockSpec((WRAP_B, cb), lambda c, i: ((seq_of(i) + 1) * (seq // WRAP_B) - 1, c)),
                  state,
                  pl.BlockSpec((CONV_A, cb), lambda c, i: (0, c))],
        out_specs=[tile,
                   pl.BlockSpec((None, CONV_A - 1, cb), lambda c, i: (seq_of(i), 0, c)),
                   pl.BlockSpec((None, CONV_B - 1, cb), lambda c, i: (seq_of(i), 0, c)),
                   state],
        out_shape=[jax.ShapeDtypeStruct((t, D), BF16),
                   jax.ShapeDtypeStruct((n_seq, CONV_A - 1, D), F32),
                   jax.ShapeDtypeStruct((n_seq, CONV_B - 1, D), F32),
                   jax.ShapeDtypeStruct((CONV_A - 1, n_batch, D), F32)],
        scratch_shapes=[pltpu.VMEM((HIST_A + tc, cb), F32)],
        compiler_params=_params("arbitrary", "arbitrary"),
        name="conv_a",
    )(cv, ba, cv, u, state_a_tm, conv_a_w)


def _mix_out_kernel(cb_ref, lg_ref, lb_ref, ya_ref, ga_ref, gb_ref, woa_ref, wob_ref, wo_ref, x_ref, o_ref, z_scr):
    c = cb_ref[...]
    mu = jnp.mean(c, axis=-1, keepdims=True)
    xc = c - mu
    var = jnp.mean(xc * xc, axis=-1, keepdims=True)
    y = xc * lax.rsqrt(var + LN_EPS) * lg_ref[...] + lb_ref[...]
    z_scr[...] = (y * jax.nn.sigmoid(y)).astype(BF16)

    y_a = jnp.dot(ya_ref[...], woa_ref[...], preferred_element_type=F32)
    y_b = jnp.dot(z_scr[...], wob_ref[...], preferred_element_type=F32)
    merged = (ga_ref[...].astype(F32) * y_a + gb_ref[...].astype(F32) * y_b).astype(BF16)
    o_ref[...] = x_ref[...] + jnp.dot(merged, wo_ref[...], preferred_element_type=F32)


def _mix_out(conv_b, ln_g, ln_b, ya_in, ga, gb, w_out_a, w_out_b, w_o, layer, x, *, tm):
    t = conv_b.shape[0]
    row = pl.BlockSpec((tm, D), lambda m: (m, 0))
    vec = pl.BlockSpec((1, D), lambda m: (0, 0))
    wfull = pl.BlockSpec((None, D, D), lambda m: (layer, 0, 0), pipeline_mode=pl.Buffered(1))
    return pl.pallas_call(
        _mix_out_kernel,
        grid=(t // tm,),
        in_specs=[row, vec, vec, row, row, row, wfull, wfull, wfull, row],
        out_specs=row,
        out_shape=jax.ShapeDtypeStruct((t, D), F32),
        scratch_shapes=[pltpu.VMEM((tm, D), BF16)],
        compiler_params=_params("arbitrary"),
        name="mix_out",
    )(conv_b, ln_g, ln_b, ya_in, ga, gb, w_out_a, w_out_b, w_o, x)


def _ffn_tile(x_ref, g_ref, wg_ref, wu_ref, wd_ref, gf_ref, o_ref, h_scr, final_norm):
    j = pl.program_id(1)

    @pl.when(j == 0)
    def _():
        x = x_ref[...]
        h_scr[...] = _rms(x, g_ref[...]).astype(BF16)
        o_ref[...] = x

    h = h_scr[...]
    gate = jnp.dot(h, wg_ref[...].astype(BF16), preferred_element_type=F32)
    up = jnp.dot(h, wu_ref[...].astype(BF16), preferred_element_type=F32)
    act = (gate * jax.nn.sigmoid(gate) * up).astype(BF16)
    o_ref[...] += jnp.dot(act, wd_ref[...].astype(BF16), preferred_element_type=F32)

    if final_norm:
        @pl.when(j == pl.num_programs(1) - 1)
        def _():
            o_ref[...] = _rms(o_ref[...], gf_ref[...])


def _ffn_kernel(x_ref, g_ref, wg_ref, wu_ref, wd_ref, gf_ref, o_ref, h_scr):
    _ffn_tile(x_ref, g_ref, wg_ref, wu_ref, wd_ref, gf_ref, o_ref, h_scr, False)


def _ffn_final_kernel(x_ref, g_ref, wg_ref, wu_ref, wd_ref, gf_ref, op_ref, os_ref, h_scr, *, n_prompt_tiles):
    m = pl.program_id(0)

    @pl.when(m < n_prompt_tiles)
    def _():
        _ffn_tile(x_ref, g_ref, wg_ref, wu_ref, wd_ref, gf_ref, op_ref, h_scr, True)

    @pl.when(m >= n_prompt_tiles)
    def _():
        _ffn_tile(x_ref, g_ref, wg_ref, wu_ref, wd_ref, gf_ref, os_ref, h_scr, True)


def _ffn(x, g, w_gate, w_up, w_down, layer, g_final, *, tm, tf, n_prompt=None):
    t = x.shape[0]
    vec = pl.BlockSpec((1, D), lambda m, j: (0, 0))
    if n_prompt is None:
        body = _ffn_kernel
        out_specs = pl.BlockSpec((tm, D), lambda m, j: (m, 0))
        out_shape = jax.ShapeDtypeStruct((t, D), F32)
    else:
        npt = n_prompt // tm
        body = functools.partial(_ffn_final_kernel, n_prompt_tiles=npt)
        out_specs = [pl.BlockSpec((tm, D), lambda m, j: (jnp.minimum(m, npt - 1), 0)),
                     pl.BlockSpec((tm, D), lambda m, j: (jnp.maximum(m - npt, 0), 0))]
        out_shape = [jax.ShapeDtypeStruct((n_prompt, D), F32), jax.ShapeDtypeStruct((t - n_prompt, D), F32)]
    return pl.pallas_call(
        body,
        grid=(t // tm, D_FF // tf),
        in_specs=[pl.BlockSpec((tm, D), lambda m, j: (m, 0), pipeline_mode=pl.Buffered(1)), vec,
                  pl.BlockSpec((None, D, tf), lambda m, j: (layer, 0, j)),
                  pl.BlockSpec((None, D, tf), lambda m, j: (layer, 0, j)),
                  pl.BlockSpec((None, tf, D), lambda m, j: (layer, j, 0)),
                  vec],
        out_specs=out_specs,
        out_shape=out_shape,
        scratch_shapes=[pltpu.VMEM((tm, D), BF16)],
        compiler_params=_params("arbitrary", "arbitrary"),
        name="ffn",
    )(x, g, w_gate, w_up, w_down, g_final)


def _cast_kernel(w_ref, o_ref):
    o_ref[...] = w_ref[...].astype(BF16)


def _cast_cols(w, n_cols, *, rows):
    depth, n_rows, _ = w.shape
    blk = pl.BlockSpec((None, rows, n_cols), lambda l, r: (l, r, 0))
    return pl.pallas_call(
        _cast_kernel,
        grid=(depth, n_rows // rows),
        in_specs=[blk],
        out_specs=blk,
        out_shape=jax.ShapeDtypeStruct((depth, n_rows, n_cols), BF16),
        compiler_params=_params("arbitrary", "arbitrary"),
        name="cast_cols",
    )(w)


def kernel(x_prompt, x_sample, state_conv_a, state_conv_b, norm_mix_g, w_in, conv_a_w, w_out_a, conv_b_w,
           conv_b_bias, ln_b_g, ln_b_b, w_out_b, w_o, norm_ffn_g, w_gate, w_up, w_down, final_norm_g):
    n_seq, seq, _ = x_prompt.shape
    n_batch, n_dec, _ = x_sample.shape
    depth = w_in.shape[0]
    n_prompt = n_seq * seq
    n_sample = n_batch * n_dec
    seg = seq // SUBLANES
    shape = dict(n_seq=n_seq, seq=seq, n_batch=n_batch)

    x = jnp.concatenate(
        [x_prompt.reshape(n_seq, SUBLANES, seg, D).transpose(0, 2, 1, 3).reshape(n_prompt, D),
         x_sample.transpose(1, 0, 2).reshape(n_sample, D)], axis=0)
    w_a = _cast_cols(w_in, 3 * D, rows=256)
    w_out_a, w_out_b, w_o = (w.astype(BF16) for w in (w_out_a, w_out_b, w_o))
    g_final = final_norm_g.reshape(1, D)
    pa, pb, sa, sb = [], [], [], []
    for l in range(depth):
        vec = lambda v: v[l].reshape(1, D)
        u, ga, gb = _in_proj_b(x, vec(norm_mix_g), w_in, l, tm=n_sample, tn=256)
        ba, cv, conv_b, new_sb = _in_proj_a(
            x, vec(norm_mix_g), w_a, l, u, state_conv_b[l].transpose(1, 0, 2), conv_b_w[l], vec(conv_b_bias),
            tm=n_sample, tn=256, **shape)
        ya_in, new_pa, new_pb, new_sa = _conv_a(cv, ba, u, state_conv_a[l].transpose(1, 0, 2), conv_a_w[l],
                                                tc=n_sample, cb=512, **shape)
        x = _mix_out(conv_b, vec(ln_b_g), vec(ln_b_b), ya_in, ga, gb, w_out_a, w_out_b, w_o, l, x, tm=256)
        x = _ffn(x, vec(norm_ffn_g), w_gate, w_up, w_down, l, g_final,
                 tm=1024, tf=256, n_prompt=(n_prompt if l == depth - 1 else None))
        pa.append(new_pa)
        pb.append(new_pb)
        sa.append(new_sa.transpose(1, 0, 2))
        sb.append(new_sb.transpose(1, 0, 2))

    y_p, y_s = x
    y_prompt = y_p.reshape(n_seq, seg, SUBLANES, D).transpose(0, 2, 1, 3).reshape(n_seq, seq, D)
    y_sample = y_s.reshape(n_dec, n_batch, D).transpose(1, 0, 2)
    return (y_prompt, y_sample, jnp.stack(pa), jnp.stack(pb), jnp.stack(sa), jnp.stack(sb))
```

```python
import functools

import jax
import jax.numpy as jnp
from jax import lax
from jax.experimental import pallas as pl
from jax.experimental.pallas import tpu as pltpu

D = 2048
D_FF = 5632
CONV_A = 3
CONV_B = 31
RMS_EPS = 1e-6
LN_EPS = 1e-5
SUBLANES = 8
MXU_SLICE_ROWS = 512
HIST_B = (CONV_B - 1) * SUBLANES
HIST_A = (CONV_A - 1) * SUBLANES
CONV_CHUNK = 32
WRAP_B = 256
WRAP_A = 16
VMEM_LIMIT = 58 * 1024 * 1024

BF16 = jnp.bfloat16
F32 = jnp.float32


def _params(*sem):
    return pltpu.CompilerParams(dimension_semantics=sem, vmem_limit_bytes=VMEM_LIMIT)


def _rms(x, g):
    return x * lax.rsqrt(jnp.mean(x * x, axis=-1, keepdims=True) + RMS_EPS) * g


def _prev_segment(rows):
    moved = pltpu.roll(rows, shift=1, axis=0)
    sub = lax.broadcasted_iota(jnp.int32, rows.shape, 0)
    return jnp.where(sub == 0, 0.0, moved)


def _in_proj_b_kernel(x_ref, g_ref, wga, wgb, wta, wtb, u_o, ga_o, gb_o, h_o):
    @pl.when(pl.program_id(1) == 0)
    def _():
        h_o[...] = _rms(x_ref[...], g_ref[...]).astype(BF16)

    h = h_o[...]

    def dot(w):
        return jnp.dot(h, w[...].astype(BF16), preferred_element_type=F32)

    u_o[...] = dot(wga) * jax.nn.sigmoid(dot(wgb))
    ga_o[...] = jax.nn.sigmoid(dot(wta)).astype(BF16)
    gb_o[...] = jax.nn.sigmoid(dot(wtb)).astype(BF16)


def _in_proj_b(x, g, w_in, layer, *, tm, tn):
    t = x.shape[0]
    nb = D // tn
    w_specs = [pl.BlockSpec((None, D, tn), functools.partial(lambda m, n, grp: (layer, 0, grp * nb + n), grp=grp))
               for grp in (3, 4, 5, 6)]
    tile = pl.BlockSpec((tm, tn), lambda m, n: (m, n))
    return pl.pallas_call(
        _in_proj_b_kernel,
        grid=(t // tm, nb),
        in_specs=[pl.BlockSpec((tm, D), lambda m, n: (m, 0)),
                  pl.BlockSpec((1, D), lambda m, n: (0, 0))] + w_specs,
        out_specs=[tile] * 3 + [pl.BlockSpec((tm, D), lambda m, n: (m, 0))],
        out_shape=[jax.ShapeDtypeStruct((t, D), F32), jax.ShapeDtypeStruct((t, D), BF16),
                   jax.ShapeDtypeStruct((t, D), BF16), jax.ShapeDtypeStruct((t, D), BF16)],
        compiler_params=_params("arbitrary", "arbitrary"),
        name="in_proj_b",
    )(x, g, w_in, w_in, w_in, w_in)


def _in_proj_a_kernel(h_ref, wb, wc, wv, u_ref, hist_ref, sb_ref, wb8_ref, bias_ref,
                      ba_o, cv_o, cb_o, nsb_o, h_scr, xh_scr,
                      *, tm, n_prompt_tiles, tiles_per_seq, n_batch):
    m = pl.program_id(0)
    groups = CONV_CHUNK // SUBLANES
    n_chunks = tm // CONV_CHUNK

    @pl.when(pl.program_id(1) == 0)
    def _():
        h_scr[...] = h_ref[...]

    def interleaved(conv_chunk):
        n_slices = tm // MXU_SLICE_ROWS
        weights = (wb, wc, wv)
        pieces = [(s, k) for s in range(n_slices) for k in range(len(weights))]
        bounds = [round(p * n_chunks / len(pieces)) for p in range(len(pieces) + 1)]
        for p, (s, k) in enumerate(pieces):
            for ci in range(bounds[p], bounds[p + 1]):
                conv_chunk(ci)
            rows = pl.ds(s * MXU_SLICE_ROWS, MXU_SLICE_ROWS)
            y = jnp.dot(h_scr[rows, :], weights[k][...], preferred_element_type=F32)
            if k == 0:
                ba_o[rows, :] = y.astype(BF16)
            elif k == 1:
                cv_o[rows, :] = y
            else:
                cv_o[rows, :] = cv_o[rows, :] * y

    @pl.when(m < n_prompt_tiles)
    def _prompt():
        first = m % tiles_per_seq == 0
        w0 = hist_ref.shape[0] - HIST_B
        for j in range(CONV_B - 1):
            rows = hist_ref[pl.ds(w0 + j * SUBLANES, SUBLANES), :]
            xh_scr[pl.ds(j * SUBLANES, SUBLANES), :] = jnp.where(first, _prev_segment(rows), rows)
        bias8 = jnp.broadcast_to(bias_ref[...], (SUBLANES, bias_ref.shape[1]))

        def conv_chunk(ci):
            c0 = ci * CONV_CHUNK
            accs = [bias8] * groups
            for k in range(CONV_B):
                w = wb8_ref[k]
                for gi in range(groups):
                    r = c0 + (gi + k) * SUBLANES - HIST_B
                    if r < 0:
                        src = xh_scr[pl.ds(r + HIST_B, SUBLANES), :]
                    else:
                        src = u_ref[pl.ds(r, SUBLANES), :]
                    accs[gi] = accs[gi] + src * w
            for gi in range(groups):
                cb_o[pl.ds(c0 + gi * SUBLANES, SUBLANES), :] = accs[gi]
            return accs[-1]

        interleaved(conv_chunk)

    @pl.when(m == n_prompt_tiles)
    def _sample():
        n_t = tm // n_batch
        bias8 = jnp.broadcast_to(bias_ref[...], (SUBLANES, bias_ref.shape[1]))

        def conv_chunk(ci):
            t, r0 = divmod(ci * CONV_CHUNK, n_batch)
            accs = [bias8] * groups
            for k in range(CONV_B):
                j = t + k
                w = wb8_ref[k]
                for gi in range(groups):
                    g0 = r0 + gi * SUBLANES
                    if j < CONV_B - 1:
                        src = sb_ref[j, pl.ds(g0, SUBLANES), :]
                    else:
                        src = u_ref[pl.ds((j - (CONV_B - 1)) * n_batch + g0, SUBLANES), :]
                    accs[gi] = accs[gi] + src * w
            for gi in range(groups):
                cb_o[pl.ds(t * n_batch + r0 + gi * SUBLANES, SUBLANES), :] = accs[gi]
            return accs[-1]

        interleaved(conv_chunk)
        for j in range(CONV_B - 1):
            jj = j + n_t
            if jj < CONV_B - 1:
                nsb_o[j] = sb_ref[jj]
            else:
                nsb_o[j] = u_ref[pl.ds((jj - (CONV_B - 1)) * n_batch, n_batch), :]


def _in_proj_a(h, w_a, layer, u, state_b_tm, conv_b_w, conv_b_bias, *, tm, tn, n_seq, seq, n_batch):
    t = h.shape[0]
    nb = D // tn
    n_prompt_tiles = (n_seq * seq) // tm
    tiles_per_seq = seq // tm
    assert t - n_seq * seq == tm and tm % n_batch == 0 and seq % tm == 0 and tm % WRAP_B == 0
    per_tile = tm // WRAP_B

    def hist_index(m, n):
        seq_last = (jnp.minimum(m // tiles_per_seq, n_seq - 1) + 1) * (seq // WRAP_B) - 1
        return jnp.where(m % tiles_per_seq == 0, seq_last, m * per_tile - 1), n

    def sample_only(m, n):
        return jnp.where(m == n_prompt_tiles, n, 0)

    w_specs = [pl.BlockSpec((None, D, tn), lambda m, n: (layer, 0, n))] * len(w_a)
    tile = pl.BlockSpec((tm, tn), lambda m, n: (m, n))
    state = pl.BlockSpec((CONV_B - 1, n_batch, tn), lambda m, n: (0, 0, sample_only(m, n)))
    kern = functools.partial(_in_proj_a_kernel, tm=tm, n_prompt_tiles=n_prompt_tiles,
                             tiles_per_seq=tiles_per_seq, n_batch=n_batch)
    return pl.pallas_call(
        kern,
        grid=(t // tm, nb),
        in_specs=[pl.BlockSpec((tm, D), lambda m, n: (m, 0))] + w_specs + [
                  tile,
                  pl.BlockSpec((WRAP_B, tn), hist_index),
                  state,
                  pl.BlockSpec((CONV_B, SUBLANES, tn), lambda m, n: (0, 0, n)),
                  pl.BlockSpec((1, tn), lambda m, n: (0, n))],
        out_specs=[tile, tile, tile, state],
        out_shape=[jax.ShapeDtypeStruct((t, D), BF16), jax.ShapeDtypeStruct((t, D), F32),
                   jax.ShapeDtypeStruct((t, D), F32),
                   jax.ShapeDtypeStruct((CONV_B - 1, n_batch, D), F32)],
        scratch_shapes=[pltpu.VMEM((tm, D), BF16), pltpu.VMEM((HIST_B, tn), F32)],
        compiler_params=_params("arbitrary", "arbitrary"),
        name="in_proj_a",
    )(h, *w_a, u, u, state_b_tm,
      jnp.broadcast_to(conv_b_w[:, None, :], (CONV_B, SUBLANES, D)), conv_b_bias)


def _conv_a_kernel(cv_ref, ba_ref, cw_ref, uw_ref, sa_ref, wa_ref, ya_o, pa_o, pb_o, nsa_o, xc_scr,
                   *, tc, n_prompt_tiles, tiles_per_seq, n_batch):
    i = pl.program_id(1)

    def taps(src_of_tap, rows):
        acc = None
        for k in range(CONV_A):
            term = src_of_tap(k) * wa_ref[k:k + 1, :]
            acc = term if acc is None else acc + term
        ya_o[rows, :] = (ba_ref[rows, :].astype(F32) * acc).astype(BF16)

    @pl.when(i < n_prompt_tiles)
    def _prompt():
        @pl.when(i % tiles_per_seq == 0)
        def _():
            w0 = cw_ref.shape[0] - HIST_A
            for j in range(CONV_A - 1):
                xc_scr[pl.ds(j * SUBLANES, SUBLANES), :] = _prev_segment(
                    cw_ref[pl.ds(w0 + j * SUBLANES, SUBLANES), :])

        @pl.when(i % tiles_per_seq != 0)
        def _():
            xc_scr[0:HIST_A, :] = xc_scr[tc:tc + HIST_A, :]

        xc_scr[HIST_A:HIST_A + tc, :] = cv_ref[...]

        def chunk(ci, carry):
            r0 = pl.multiple_of(ci * CONV_CHUNK, CONV_CHUNK)
            taps(lambda k: xc_scr[pl.ds(r0 + k * SUBLANES, CONV_CHUNK), :], pl.ds(r0, CONV_CHUNK))
            return carry

        lax.fori_loop(0, tc // CONV_CHUNK, chunk, 0)

        @pl.when(i % tiles_per_seq == tiles_per_seq - 1)
        def _():
            last = SUBLANES - 1
            for j in range(CONV_A - 1):
                pa_o[pl.ds(j, 1), :] = cv_ref[pl.ds(tc - HIST_A + j * SUBLANES + last, 1), :]
            for j in range(CONV_B - 1):
                pb_o[pl.ds(j, 1), :] = uw_ref[pl.ds(uw_ref.shape[0] - HIST_B + j * SUBLANES + last, 1), :]

    @pl.when(i == n_prompt_tiles)
    def _sample():
        n_t = tc // n_batch
        for t in range(n_t):
            def src_of_tap(k, t=t):
                j = t + k
                if j < CONV_A - 1:
                    return sa_ref[j]
                return cv_ref[pl.ds((j - (CONV_A - 1)) * n_batch, n_batch), :]

            taps(src_of_tap, pl.ds(t * n_batch, n_batch))
        for j in range(CONV_A - 1):
            jj = j + n_t
            if jj < CONV_A - 1:
                nsa_o[j] = sa_ref[jj]
            else:
                nsa_o[j] = cv_ref[pl.ds((jj - (CONV_A - 1)) * n_batch, n_batch), :]


def _conv_a(cv, ba, u, state_a_tm, conv_a_w, *, tc, cb, n_seq, seq, n_batch):
    t = cv.shape[0]
    n_prompt_tiles = (n_seq * seq) // tc
    tiles_per_seq = seq // tc
    assert t - n_seq * seq == tc and tc % n_batch == 0 and seq % tc == 0 and tc % CONV_CHUNK == 0

    def seq_of(i):
        return jnp.minimum(i // tiles_per_seq, n_seq - 1)

    tile = pl.BlockSpec((tc, cb), lambda c, i: (i, c))
    state = pl.BlockSpec((CONV_A - 1, n_batch, cb), lambda c, i: (0, 0, c))
    kern = functools.partial(_conv_a_kernel, tc=tc, n_prompt_tiles=n_prompt_tiles, tiles_per_seq=tiles_per_seq,
                             n_batch=n_batch)
    return pl.pallas_call(
        kern,
        grid=(D // cb, t // tc),
        in_specs=[tile, tile,
                  pl.BlockSpec((WRAP_A, cb), lambda c, i: ((seq_of(i) + 1) * (seq // WRAP_A) - 1, c)),
                  pl.BlockSpec((WRAP_B, cb), lambda c, i: ((seq_of(i) + 1) * (seq // WRAP_B) - 1, c)),
                  state,
                  pl.BlockSpec((CONV_A, cb), lambda c, i: (0, c))],
        out_specs=[tile,
                   pl.BlockSpec((None, CONV_A - 1, cb), lambda c, i: (seq_of(i), 0, c)),
                   pl.BlockSpec((None, CONV_B - 1, cb), lambda c, i: (seq_of(i), 0, c)),
                   state],
        out_shape=[jax.ShapeDtypeStruct((t, D), BF16),
                   jax.ShapeDtypeStruct((n_seq, CONV_A - 1, D), F32),
                   jax.ShapeDtypeStruct((n_seq, CONV_B - 1, D), F32),
                   jax.ShapeDtypeStruct((CONV_A - 1, n_batch, D), F32)],
        scratch_shapes=[pltpu.VMEM((HIST_A + tc, cb), F32)],
        compiler_params=_params("arbitrary", "arbitrary"),
        name="conv_a",
    )(cv, ba, cv, u, state_a_tm, conv_a_w)


def _mix_out_kernel(cb_ref, lg_ref, lb_ref, ya_ref, ga_ref, gb_ref, woa_ref, wob_ref, wo_ref, x_ref, o_ref, z_scr):
    c = cb_ref[...]
    mu = jnp.mean(c, axis=-1, keepdims=True)
    xc = c - mu
    var = jnp.mean(xc * xc, axis=-1, keepdims=True)
    y = xc * lax.rsqrt(var + LN_EPS) * lg_ref[...] + lb_ref[...]
    z_scr[...] = (y * jax.nn.sigmoid(y)).astype(BF16)

    y_a = jnp.dot(ya_ref[...], woa_ref[...], preferred_element_type=F32)
    y_b = jnp.dot(z_scr[...], wob_ref[...], preferred_element_type=F32)
    merged = (ga_ref[...].astype(F32) * y_a + gb_ref[...].astype(F32) * y_b).astype(BF16)
    o_ref[...] = x_ref[...] + jnp.dot(merged, wo_ref[...], preferred_element_type=F32)


def _mix_out(conv_b, ln_g, ln_b, ya_in, ga, gb, w_out_a, w_out_b, w_o, layer, x, *, tm):
    t = conv_b.shape[0]
    row = pl.BlockSpec((tm, D), lambda m: (m, 0))
    vec = pl.BlockSpec((1, D), lambda m: (0, 0))
    wfull = pl.BlockSpec((None, D, D), lambda m: (layer, 0, 0), pipeline_mode=pl.Buffered(1))
    return pl.pallas_call(
        _mix_out_kernel,
        grid=(t // tm,),
        in_specs=[row, vec, vec, row, row, row, wfull, wfull, wfull, row],
        out_specs=row,
        out_shape=jax.ShapeDtypeStruct((t, D), F32),
        scratch_shapes=[pltpu.VMEM((tm, D), BF16)],
        compiler_params=_params("arbitrary"),
        name="mix_out",
    )(conv_b, ln_g, ln_b, ya_in, ga, gb, w_out_a, w_out_b, w_o, x)


def _ffn_tile(x_ref, g_ref, wg_ref, wu_ref, wd_ref, gf_ref, o_ref, h_scr, final_norm):
    j = pl.program_id(1)

    @pl.when(j == 0)
    def _():
        x = x_ref[...]
        h_scr[...] = _rms(x, g_ref[...]).astype(BF16)
        o_ref[...] = x

    h = h_scr[...]
    gate = jnp.dot(h, wg_ref[...].astype(BF16), preferred_element_type=F32)
    up = jnp.dot(h, wu_ref[...].astype(BF16), preferred_element_type=F32)
    act = (gate * jax.nn.sigmoid(gate) * up).astype(BF16)
    o_ref[...] += jnp.dot(act, wd_ref[...].astype(BF16), preferred_element_type=F32)

    if final_norm:
        @pl.when(j == pl.num_programs(1) - 1)
        def _():
            o_ref[...] = _rms(o_ref[...], gf_ref[...])


def _ffn_kernel(x_ref, g_ref, wg_ref, wu_ref, wd_ref, gf_ref, o_ref, h_scr):
    _ffn_tile(x_ref, g_ref, wg_ref, wu_ref, wd_ref, gf_ref, o_ref, h_scr, False)


def _ffn_final_kernel(x_ref, g_ref, wg_ref, wu_ref, wd_ref, gf_ref, op_ref, os_ref, h_scr, *, n_prompt_tiles):
    m = pl.program_id(0)

    @pl.when(m < n_prompt_tiles)
    def _():
        _ffn_tile(x_ref, g_ref, wg_ref, wu_ref, wd_ref, gf_ref, op_ref, h_scr, True)

    @pl.when(m >= n_prompt_tiles)
    def _():
        _ffn_tile(x_ref, g_ref, wg_ref, wu_ref, wd_ref, gf_ref, os_ref, h_scr, True)


def _ffn(x, g, w_gate, w_up, w_down, layer, g_final, *, tm, tf, n_prompt=None):
    t = x.shape[0]
    vec = pl.BlockSpec((1, D), lambda m, j: (0, 0))
    if n_prompt is None:
        body = _ffn_kernel
        x_spec = pl.BlockSpec((tm, D), lambda m, j: (m, 0))
        out_specs = pl.BlockSpec((tm, D), lambda m, j: (m, 0))
        out_shape = jax.ShapeDtypeStruct((t, D), F32)
    else:
        npt = n_prompt // tm
        body = functools.partial(_ffn_final_kernel, n_prompt_tiles=npt)
        x_spec = pl.BlockSpec((tm, D), lambda m, j: (m, 0), pipeline_mode=pl.Buffered(1))
        out_specs = [pl.BlockSpec((tm, D), lambda m, j: (jnp.minimum(m, npt - 1), 0)),
                     pl.BlockSpec((tm, D), lambda m, j: (jnp.maximum(m - npt, 0), 0))]
        out_shape = [jax.ShapeDtypeStruct((n_prompt, D), F32), jax.ShapeDtypeStruct((t - n_prompt, D), F32)]
    return pl.pallas_call(
        body,
        grid=(t // tm, D_FF // tf),
        in_specs=[x_spec, vec,
                  pl.BlockSpec((None, D, tf), lambda m, j: (layer, 0, j)),
                  pl.BlockSpec((None, D, tf), lambda m, j: (layer, 0, j)),
                  pl.BlockSpec((None, tf, D), lambda m, j: (layer, j, 0)),
                  vec],
        out_specs=out_specs,
        out_shape=out_shape,
        scratch_shapes=[pltpu.VMEM((tm, D), BF16)],
        compiler_params=_params("arbitrary", "arbitrary"),
        name="ffn",
    )(x, g, w_gate, w_up, w_down, g_final)


def _cast_kernel(*refs):
    n = len(refs) // 2
    for w_ref, o_ref in zip(refs[:n], refs[n:]):
        o_ref[...] = w_ref[...].astype(BF16)


def _cast_groups(w, n_groups, *, rows):
    depth, n_rows, _ = w.shape
    return pl.pallas_call(
        _cast_kernel,
        grid=(depth, n_rows // rows),
        in_specs=[pl.BlockSpec((None, rows, D), functools.partial(lambda l, r, grp: (l, r, grp), grp=grp))
                  for grp in range(n_groups)],
        out_specs=[pl.BlockSpec((None, rows, D), lambda l, r: (l, r, 0))] * n_groups,
        out_shape=[jax.ShapeDtypeStruct((depth, n_rows, D), BF16)] * n_groups,
        compiler_params=_params("arbitrary", "arbitrary"),
        name="cast_groups",
    )(*([w] * n_groups))


def kernel(x_prompt, x_sample, state_conv_a, state_conv_b, norm_mix_g, w_in, conv_a_w, w_out_a, conv_b_w,
           conv_b_bias, ln_b_g, ln_b_b, w_out_b, w_o, norm_ffn_g, w_gate, w_up, w_down, final_norm_g):
    n_seq, seq, _ = x_prompt.shape
    n_batch, n_dec, _ = x_sample.shape
    depth = w_in.shape[0]
    n_prompt = n_seq * seq
    n_sample = n_batch * n_dec
    seg = seq // SUBLANES
    shape = dict(n_seq=n_seq, seq=seq, n_batch=n_batch)

    x = jnp.concatenate(
        [x_prompt.reshape(n_seq, SUBLANES, seg, D).transpose(0, 2, 1, 3).reshape(n_prompt, D),
         x_sample.transpose(1, 0, 2).reshape(n_sample, D)], axis=0)
    w_a = _cast_groups(w_in, 3, rows=512)
    w_out_a, w_out_b, w_o = (w.astype(BF16) for w in (w_out_a, w_out_b, w_o))
    g_final = final_norm_g.reshape(1, D)
    pa, pb, sa, sb = [], [], [], []
    for l in range(depth):
        vec = lambda v: v[l].reshape(1, D)
        u, ga, gb, h = _in_proj_b(x, vec(norm_mix_g), w_in, l, tm=n_sample, tn=256)
        ba, cv, conv_b, new_sb = _in_proj_a(
            h, w_a, l, u, state_conv_b[l].transpose(1, 0, 2), conv_b_w[l], vec(conv_b_bias),
            tm=n_sample, tn=256, **shape)
        ya_in, new_pa, new_pb, new_sa = _conv_a(cv, ba, u, state_conv_a[l].transpose(1, 0, 2), conv_a_w[l],
                                                tc=n_sample, cb=512, **shape)
        x = _mix_out(conv_b, vec(ln_b_g), vec(ln_b_b), ya_in, ga, gb, w_out_a, w_out_b, w_o, l, x, tm=256)
        x = _ffn(x, vec(norm_ffn_g), w_gate, w_up, w_down, l, g_final,
                 tm=1024, tf=256, n_prompt=(n_prompt if l == depth - 1 else None))
        pa.append(new_pa)
        pb.append(new_pb)
        sa.append(new_sa.transpose(1, 0, 2))
        sb.append(new_sb.transpose(1, 0, 2))

    y_p, y_s = x
    y_prompt = y_p.reshape(n_seq, seg, SUBLANES, D).transpose(0, 2, 1, 3).reshape(n_seq, seq, D)
    y_sample = y_s.reshape(n_dec, n_batch, D).transpose(1, 0, 2)
    return (y_prompt, y_sample, jnp.stack(pa), jnp.stack(pb), jnp.stack(sa), jnp.stack(sb))
```

```python
import functools

import jax
import jax.numpy as jnp
from jax import lax
from jax.experimental import pallas as pl
from jax.experimental.pallas import tpu as pltpu

D = 2048
D_FF = 5632
CONV_A = 3
CONV_B = 31
RMS_EPS = 1e-6
LN_EPS = 1e-5
SUBLANES = 8
MXU_SLICE_ROWS = 512
HIST_B = (CONV_B - 1) * SUBLANES
HIST_A = (CONV_A - 1) * SUBLANES
CONV_CHUNK = 32
WRAP_B = 256
WRAP_A = 16
VMEM_LIMIT = 58 * 1024 * 1024

BF16 = jnp.bfloat16
F32 = jnp.float32


def _params(*sem):
    return pltpu.CompilerParams(dimension_semantics=sem, vmem_limit_bytes=VMEM_LIMIT)


def _rms(x, g):
    return x * lax.rsqrt(jnp.mean(x * x, axis=-1, keepdims=True) + RMS_EPS) * g


def _prev_segment(rows):
    moved = pltpu.roll(rows, shift=1, axis=0)
    sub = lax.broadcasted_iota(jnp.int32, rows.shape, 0)
    return jnp.where(sub == 0, 0.0, moved)


def _in_proj_b_kernel(x_ref, g_ref, wga, wgb, wta, wtb, u_o, ga_o, gb_o, h_o):
    @pl.when(pl.program_id(1) == 0)
    def _():
        h_o[...] = _rms(x_ref[...], g_ref[...]).astype(BF16)

    h = h_o[...]

    def dot(w):
        return jnp.dot(h, w[...].astype(BF16), preferred_element_type=F32)

    ga_o[...] = jax.nn.sigmoid(dot(wta)).astype(BF16)
    gb_o[...] = jax.nn.sigmoid(dot(wtb)).astype(BF16)
    glu_gate = jax.nn.sigmoid(dot(wgb))
    u_o[...] = dot(wga) * glu_gate


def _in_proj_b(x, g, w_in, layer, *, tm, tn):
    t = x.shape[0]
    nb = D // tn
    w_specs = [pl.BlockSpec((None, D, tn), functools.partial(lambda m, n, grp: (layer, 0, grp * nb + n), grp=grp))
               for grp in (3, 4, 5, 6)]
    tile = pl.BlockSpec((tm, tn), lambda m, n: (m, n))
    return pl.pallas_call(
        _in_proj_b_kernel,
        grid=(t // tm, nb),
        in_specs=[pl.BlockSpec((tm, D), lambda m, n: (m, 0)),
                  pl.BlockSpec((1, D), lambda m, n: (0, 0))] + w_specs,
        out_specs=[tile] * 3 + [pl.BlockSpec((tm, D), lambda m, n: (m, 0))],
        out_shape=[jax.ShapeDtypeStruct((t, D), F32), jax.ShapeDtypeStruct((t, D), BF16),
                   jax.ShapeDtypeStruct((t, D), BF16), jax.ShapeDtypeStruct((t, D), BF16)],
        compiler_params=_params("arbitrary", "arbitrary"),
        name="in_proj_b",
    )(x, g, w_in, w_in, w_in, w_in)


def _in_proj_a_kernel(h_ref, wb, wc, wv, u_ref, hist_ref, sb_ref, wb8_ref, bias_ref,
                      ba_o, cv_o, cb_o, nsb_o, h_scr, xh_scr,
                      *, tm, n_prompt_tiles, tiles_per_seq, n_batch):
    m = pl.program_id(0)
    groups = CONV_CHUNK // SUBLANES
    n_chunks = tm // CONV_CHUNK

    @pl.when(pl.program_id(1) == 0)
    def _():
        h_scr[...] = h_ref[...]

    def interleaved(conv_chunk):
        n_slices = tm // MXU_SLICE_ROWS
        weights = (wb, wc, wv)
        pieces = [(s, k) for s in range(n_slices) for k in range(len(weights))]
        bounds = [round(p * n_chunks / len(pieces)) for p in range(len(pieces) + 1)]
        for p, (s, k) in enumerate(pieces):
            for ci in range(bounds[p], bounds[p + 1]):
                conv_chunk(ci)
            rows = pl.ds(s * MXU_SLICE_ROWS, MXU_SLICE_ROWS)
            y = jnp.dot(h_scr[rows, :], weights[k][...], preferred_element_type=F32)
            if k == 0:
                ba_o[rows, :] = y.astype(BF16)
            elif k == 1:
                cv_o[rows, :] = y
            else:
                cv_o[rows, :] = cv_o[rows, :] * y

    @pl.when(m < n_prompt_tiles)
    def _prompt():
        first = m % tiles_per_seq == 0
        w0 = hist_ref.shape[0] - HIST_B
        for j in range(CONV_B - 1):
            rows = hist_ref[pl.ds(w0 + j * SUBLANES, SUBLANES), :]
            xh_scr[pl.ds(j * SUBLANES, SUBLANES), :] = jnp.where(first, _prev_segment(rows), rows)
        bias8 = jnp.broadcast_to(bias_ref[...], (SUBLANES, bias_ref.shape[1]))

        def conv_chunk(ci):
            c0 = ci * CONV_CHUNK
            accs = [bias8] * groups
            for k in range(CONV_B):
                w = wb8_ref[k]
                for gi in range(groups):
                    r = c0 + (gi + k) * SUBLANES - HIST_B
                    if r < 0:
                        src = xh_scr[pl.ds(r + HIST_B, SUBLANES), :]
                    else:
                        src = u_ref[pl.ds(r, SUBLANES), :]
                    accs[gi] = accs[gi] + src * w
            for gi in range(groups):
                cb_o[pl.ds(c0 + gi * SUBLANES, SUBLANES), :] = accs[gi]
            return accs[-1]

        interleaved(conv_chunk)

    @pl.when(m == n_prompt_tiles)
    def _sample():
        n_t = tm // n_batch
        bias8 = jnp.broadcast_to(bias_ref[...], (SUBLANES, bias_ref.shape[1]))

        def conv_chunk(ci):
            t, r0 = divmod(ci * CONV_CHUNK, n_batch)
            accs = [bias8] * groups
            for k in range(CONV_B):
                j = t + k
                w = wb8_ref[k]
                for gi in range(groups):
                    g0 = r0 + gi * SUBLANES
                    if j < CONV_B - 1:
                        src = sb_ref[j, pl.ds(g0, SUBLANES), :]
                    else:
                        src = u_ref[pl.ds((j - (CONV_B - 1)) * n_batch + g0, SUBLANES), :]
                    accs[gi] = accs[gi] + src * w
            for gi in range(groups):
                cb_o[pl.ds(t * n_batch + r0 + gi * SUBLANES, SUBLANES), :] = accs[gi]
            return accs[-1]

        interleaved(conv_chunk)
        for j in range(CONV_B - 1):
            jj = j + n_t
            if jj < CONV_B - 1:
                nsb_o[j] = sb_ref[jj]
            else:
                nsb_o[j] = u_ref[pl.ds((jj - (CONV_B - 1)) * n_batch, n_batch), :]


def _in_proj_a(h, w_a, layer, u, state_b_tm, conv_b_w, conv_b_bias, *, tm, tn, n_seq, seq, n_batch):
    t = h.shape[0]
    nb = D // tn
    n_prompt_tiles = (n_seq * seq) // tm
    tiles_per_seq = seq // tm
    assert t - n_seq * seq == tm and tm % n_batch == 0 and seq % tm == 0 and tm % WRAP_B == 0
    per_tile = tm // WRAP_B

    def hist_index(m, n):
        seq_last = (jnp.minimum(m // tiles_per_seq, n_seq - 1) + 1) * (seq // WRAP_B) - 1
        return jnp.where(m % tiles_per_seq == 0, seq_last, m * per_tile - 1), n

    def sample_only(m, n):
        return jnp.where(m == n_prompt_tiles, n, 0)

    w_specs = [pl.BlockSpec((None, D, tn), lambda m, n: (layer, 0, n))] * len(w_a)
    tile = pl.BlockSpec((tm, tn), lambda m, n: (m, n))
    state = pl.BlockSpec((CONV_B - 1, n_batch, tn), lambda m, n: (0, 0, sample_only(m, n)))
    kern = functools.partial(_in_proj_a_kernel, tm=tm, n_prompt_tiles=n_prompt_tiles,
                             tiles_per_seq=tiles_per_seq, n_batch=n_batch)
    return pl.pallas_call(
        kern,
        grid=(t // tm, nb),
        in_specs=[pl.BlockSpec((tm, D), lambda m, n: (m, 0))] + w_specs + [
                  tile,
                  pl.BlockSpec((WRAP_B, tn), hist_index),
                  state,
                  pl.BlockSpec((CONV_B, SUBLANES, tn), lambda m, n: (0, 0, n)),
                  pl.BlockSpec((1, tn), lambda m, n: (0, n))],
        out_specs=[tile, tile, tile, state],
        out_shape=[jax.ShapeDtypeStruct((t, D), BF16), jax.ShapeDtypeStruct((t, D), F32),
                   jax.ShapeDtypeStruct((t, D), F32),
                   jax.ShapeDtypeStruct((CONV_B - 1, n_batch, D), F32)],
        scratch_shapes=[pltpu.VMEM((tm, D), BF16), pltpu.VMEM((HIST_B, tn), F32)],
        compiler_params=_params("arbitrary", "arbitrary"),
        name="in_proj_a",
    )(h, *w_a, u, u, state_b_tm,
      jnp.broadcast_to(conv_b_w[:, None, :], (CONV_B, SUBLANES, D)), conv_b_bias)


def _conv_a_kernel(cv_ref, ba_ref, cw_ref, uw_ref, sa_ref, wa_ref, ya_o, pa_o, pb_o, nsa_o, xc_scr,
                   *, tc, n_prompt_tiles, tiles_per_seq, n_batch):
    i = pl.program_id(1)

    def taps(src_of_tap, rows):
        acc = None
        for k in range(CONV_A):
            term = src_of_tap(k) * wa_ref[k:k + 1, :]
            acc = term if acc is None else acc + term
        ya_o[rows, :] = (ba_ref[rows, :].astype(F32) * acc).astype(BF16)

    @pl.when(i < n_prompt_tiles)
    def _prompt():
        @pl.when(i % tiles_per_seq == 0)
        def _():
            w0 = cw_ref.shape[0] - HIST_A
            for j in range(CONV_A - 1):
                xc_scr[pl.ds(j * SUBLANES, SUBLANES), :] = _prev_segment(
                    cw_ref[pl.ds(w0 + j * SUBLANES, SUBLANES), :])

        @pl.when(i % tiles_per_seq != 0)
        def _():
            xc_scr[0:HIST_A, :] = xc_scr[tc:tc + HIST_A, :]

        xc_scr[HIST_A:HIST_A + tc, :] = cv_ref[...]

        def chunk(ci, carry):
            r0 = pl.multiple_of(ci * CONV_CHUNK, CONV_CHUNK)
            taps(lambda k: xc_scr[pl.ds(r0 + k * SUBLANES, CONV_CHUNK), :], pl.ds(r0, CONV_CHUNK))
            return carry

        lax.fori_loop(0, tc // CONV_CHUNK, chunk, 0)

        @pl.when(i % tiles_per_seq == tiles_per_seq - 1)
        def _():
            last = SUBLANES - 1
            for j in range(CONV_A - 1):
                pa_o[pl.ds(j, 1), :] = cv_ref[pl.ds(tc - HIST_A + j * SUBLANES + last, 1), :]
            for j in range(CONV_B - 1):
                pb_o[pl.ds(j, 1), :] = uw_ref[pl.ds(uw_ref.shape[0] - HIST_B + j * SUBLANES + last, 1), :]

    @pl.when(i == n_prompt_tiles)
    def _sample():
        n_t = tc // n_batch
        for t in range(n_t):
            def src_of_tap(k, t=t):
                j = t + k
                if j < CONV_A - 1:
                    return sa_ref[j]
                return cv_ref[pl.ds((j - (CONV_A - 1)) * n_batch, n_batch), :]

            taps(src_of_tap, pl.ds(t * n_batch, n_batch))
        for j in range(CONV_A - 1):
            jj = j + n_t
            if jj < CONV_A - 1:
                nsa_o[j] = sa_ref[jj]
            else:
                nsa_o[j] = cv_ref[pl.ds((jj - (CONV_A - 1)) * n_batch, n_batch), :]


def _conv_a(cv, ba, u, state_a_tm, conv_a_w, *, tc, cb, n_seq, seq, n_batch):
    t = cv.shape[0]
    n_prompt_tiles = (n_seq * seq) // tc
    tiles_per_seq = seq // tc
    assert t - n_seq * seq == tc and tc % n_batch == 0 and seq % tc == 0 and tc % CONV_CHUNK == 0

    def seq_of(i):
        return jnp.minimum(i // tiles_per_seq, n_seq - 1)

    tile = pl.BlockSpec((tc, cb), lambda c, i: (i, c))
    state = pl.BlockSpec((CONV_A - 1, n_batch, cb), lambda c, i: (0, 0, c))
    kern = functools.partial(_conv_a_kernel, tc=tc, n_prompt_tiles=n_prompt_tiles, tiles_per_seq=tiles_per_seq,
                             n_batch=n_batch)
    return pl.pallas_call(
        kern,
        grid=(D // cb, t // tc),
        in_specs=[tile, tile,
                  pl.BlockSpec((WRAP_A, cb), lambda c, i: ((seq_of(i) + 1) * (seq // WRAP_A) - 1, c)),
                  pl.BlockSpec((WRAP_B, cb), lambda c, i: ((seq_of(i) + 1) * (seq // WRAP_B) - 1, c)),
                  state,
                  pl.BlockSpec((CONV_A, cb), lambda c, i: (0, c))],
        out_specs=[tile,
                   pl.BlockSpec((None, CONV_A - 1, cb), lambda c, i: (seq_of(i), 0, c)),
                   pl.BlockSpec((None, CONV_B - 1, cb), lambda c, i: (seq_of(i), 0, c)),
                   state],
        out_shape=[jax.ShapeDtypeStruct((t, D), BF16),
                   jax.ShapeDtypeStruct((n_seq, CONV_A - 1, D), F32),
                   jax.ShapeDtypeStruct((n_seq, CONV_B - 1, D), F32),
                   jax.ShapeDtypeStruct((CONV_A - 1, n_batch, D), F32)],
        scratch_shapes=[pltpu.VMEM((HIST_A + tc, cb), F32)],
        compiler_params=_params("arbitrary", "arbitrary"),
        name="conv_a",
    )(cv, ba, cv, u, state_a_tm, conv_a_w)


def _mix_out_kernel(cb_ref, lg_ref, lb_ref, ya_ref, ga_ref, gb_ref, woa_ref, wob_ref, wo_ref, x_ref, o_ref, z_scr):
    c = cb_ref[...]
    mu = jnp.mean(c, axis=-1, keepdims=True)
    xc = c - mu
    var = jnp.mean(xc * xc, axis=-1, keepdims=True)
    y = xc * lax.rsqrt(var + LN_EPS) * lg_ref[...] + lb_ref[...]
    z_scr[...] = (y * jax.nn.sigmoid(y)).astype(BF16)

    y_a = jnp.dot(ya_ref[...], woa_ref[...], preferred_element_type=F32)
    y_b = jnp.dot(z_scr[...], wob_ref[...], preferred_element_type=F32)
    merged = (ga_ref[...].astype(F32) * y_a + gb_ref[...].astype(F32) * y_b).astype(BF16)
    o_ref[...] = x_ref[...] + jnp.dot(merged, wo_ref[...], preferred_element_type=F32)


def _mix_out(conv_b, ln_g, ln_b, ya_in, ga, gb, w_out_a, w_out_b, w_o, layer, x, *, tm):
    t = conv_b.shape[0]
    row = pl.BlockSpec((tm, D), lambda m: (m, 0))
    vec = pl.BlockSpec((1, D), lambda m: (0, 0))
    wfull = pl.BlockSpec((None, D, D), lambda m: (layer, 0, 0), pipeline_mode=pl.Buffered(1))
    return pl.pallas_call(
        _mix_out_kernel,
        grid=(t // tm,),
        in_specs=[row, vec, vec, row, row, row, wfull, wfull, wfull, row],
        out_specs=row,
        out_shape=jax.ShapeDtypeStruct((t, D), F32),
        scratch_shapes=[pltpu.VMEM((tm, D), BF16)],
        compiler_params=_params("arbitrary"),
        name="mix_out",
    )(conv_b, ln_g, ln_b, ya_in, ga, gb, w_out_a, w_out_b, w_o, x)


def _ffn_tile(x_ref, g_ref, wg_ref, wu_ref, wd_ref, gf_ref, o_ref, h_scr, final_norm):
    j = pl.program_id(1)

    @pl.when(j == 0)
    def _():
        x = x_ref[...]
        h_scr[...] = _rms(x, g_ref[...]).astype(BF16)
        o_ref[...] = x

    h = h_scr[...]
    gate = jnp.dot(h, wg_ref[...].astype(BF16), preferred_element_type=F32)
    up = jnp.dot(h, wu_ref[...].astype(BF16), preferred_element_type=F32)
    act = (gate * jax.nn.sigmoid(gate) * up).astype(BF16)
    o_ref[...] += jnp.dot(act, wd_ref[...].astype(BF16), preferred_element_type=F32)

    if final_norm:
        @pl.when(j == pl.num_programs(1) - 1)
        def _():
            o_ref[...] = _rms(o_ref[...], gf_ref[...])


def _ffn_kernel(x_ref, g_ref, wg_ref, wu_ref, wd_ref, gf_ref, o_ref, h_scr):
    _ffn_tile(x_ref, g_ref, wg_ref, wu_ref, wd_ref, gf_ref, o_ref, h_scr, False)


def _ffn_final_kernel(x_ref, g_ref, wg_ref, wu_ref, wd_ref, gf_ref, op_ref, os_ref, h_scr, *, n_prompt_tiles):
    m = pl.program_id(0)

    @pl.when(m < n_prompt_tiles)
    def _():
        _ffn_tile(x_ref, g_ref, wg_ref, wu_ref, wd_ref, gf_ref, op_ref, h_scr, True)

    @pl.when(m >= n_prompt_tiles)
    def _():
        _ffn_tile(x_ref, g_ref, wg_ref, wu_ref, wd_ref, gf_ref, os_ref, h_scr, True)


def _ffn(x, g, w_gate, w_up, w_down, layer, g_final, *, tm, tf, n_prompt=None):
    t = x.shape[0]
    vec = pl.BlockSpec((1, D), lambda m, j: (0, 0))
    if n_prompt is None:
        body = _ffn_kernel
        out_specs = pl.BlockSpec((tm, D), lambda m, j: (m, 0))
        out_shape = jax.ShapeDtypeStruct((t, D), F32)
    else:
        npt = n_prompt // tm
        body = functools.partial(_ffn_final_kernel, n_prompt_tiles=npt)
        out_specs = [pl.BlockSpec((tm, D), lambda m, j: (jnp.minimum(m, npt - 1), 0)),
                     pl.BlockSpec((tm, D), lambda m, j: (jnp.maximum(m - npt, 0), 0),
                                  pipeline_mode=pl.Buffered(1))]
        out_shape = [jax.ShapeDtypeStruct((n_prompt, D), F32), jax.ShapeDtypeStruct((t - n_prompt, D), F32)]
    return pl.pallas_call(
        body,
        grid=(t // tm, D_FF // tf),
        in_specs=[pl.BlockSpec((tm, D), lambda m, j: (m, 0)), vec,
                  pl.BlockSpec((None, D, tf), lambda m, j: (layer, 0, j)),
                  pl.BlockSpec((None, D, tf), lambda m, j: (layer, 0, j)),
                  pl.BlockSpec((None, tf, D), lambda m, j: (layer, j, 0)),
                  vec],
        out_specs=out_specs,
        out_shape=out_shape,
        scratch_shapes=[pltpu.VMEM((tm, D), BF16)],
        compiler_params=_params("arbitrary", "arbitrary"),
        name="ffn",
    )(x, g, w_gate, w_up, w_down, g_final)


def _cast_kernel(*refs):
    n = len(refs) // 2
    for w_ref, o_ref in zip(refs[:n], refs[n:]):
        o_ref[...] = w_ref[...].astype(BF16)


def _cast_groups(w, n_groups, *, rows):
    depth, n_rows, _ = w.shape
    return pl.pallas_call(
        _cast_kernel,
        grid=(depth, n_rows // rows),
        in_specs=[pl.BlockSpec((None, rows, D), functools.partial(lambda l, r, grp: (l, r, grp), grp=grp))
                  for grp in range(n_groups)],
        out_specs=[pl.BlockSpec((None, rows, D), lambda l, r: (l, r, 0))] * n_groups,
        out_shape=[jax.ShapeDtypeStruct((depth, n_rows, D), BF16)] * n_groups,
        compiler_params=_params("arbitrary", "arbitrary"),
        name="cast_groups",
    )(*([w] * n_groups))


def kernel(x_prompt, x_sample, state_conv_a, state_conv_b, norm_mix_g, w_in, conv_a_w, w_out_a, conv_b_w,
           conv_b_bias, ln_b_g, ln_b_b, w_out_b, w_o, norm_ffn_g, w_gate, w_up, w_down, final_norm_g):
    n_seq, seq, _ = x_prompt.shape
    n_batch, n_dec, _ = x_sample.shape
    depth = w_in.shape[0]
    n_prompt = n_seq * seq
    n_sample = n_batch * n_dec
    seg = seq // SUBLANES
    shape = dict(n_seq=n_seq, seq=seq, n_batch=n_batch)

    x = jnp.concatenate(
        [x_prompt.reshape(n_seq, SUBLANES, seg, D).transpose(0, 2, 1, 3).reshape(n_prompt, D),
         x_sample.transpose(1, 0, 2).reshape(n_sample, D)], axis=0)
    w_a = _cast_groups(w_in, 3, rows=128)
    w_out_a, w_out_b, w_o = (w.astype(BF16) for w in (w_out_a, w_out_b, w_o))
    g_final = final_norm_g.reshape(1, D)
    pa, pb, sa, sb = [], [], [], []
    for l in range(depth):
        vec = lambda v: v[l].reshape(1, D)
        u, ga, gb, h = _in_proj_b(x, vec(norm_mix_g), w_in, l, tm=n_sample, tn=256)
        ba, cv, conv_b, new_sb = _in_proj_a(
            h, w_a, l, u, state_conv_b[l].transpose(1, 0, 2), conv_b_w[l], vec(conv_b_bias),
            tm=n_sample, tn=256, **shape)
        ya_in, new_pa, new_pb, new_sa = _conv_a(cv, ba, u, state_conv_a[l].transpose(1, 0, 2), conv_a_w[l],
                                                tc=n_sample, cb=512, **shape)
        x = _mix_out(conv_b, vec(ln_b_g), vec(ln_b_b), ya_in, ga, gb, w_out_a, w_out_b, w_o, l, x, tm=256)
        x = _ffn(x, vec(norm_ffn_g), w_gate, w_up, w_down, l, g_final,
                 tm=1024, tf=256, n_prompt=(n_prompt if l == depth - 1 else None))
        pa.append(new_pa)
        pb.append(new_pb)
        sa.append(new_sa.transpose(1, 0, 2))
        sb.append(new_sb.transpose(1, 0, 2))

    y_p, y_s = x
    y_prompt = y_p.reshape(n_seq, seg, SUBLANES, D).transpose(0, 2, 1, 3).reshape(n_seq, seq, D)
    y_sample = y_s.reshape(n_dec, n_batch, D).transpose(1, 0, 2)
    return (y_prompt, y_sample, jnp.stack(pa), jnp.stack(pb), jnp.stack(sa), jnp.stack(sb))
```

```python
import functools

import jax
import jax.numpy as jnp
from jax import lax
from jax.experimental import pallas as pl
from jax.experimental.pallas import tpu as pltpu

D = 2048
D_FF = 5632
CONV_A = 3
CONV_B = 31
RMS_EPS = 1e-6
LN_EPS = 1e-5
SUBLANES = 8
MXU_SLICE_ROWS = 256
HIST_B = (CONV_B - 1) * SUBLANES
HIST_A = (CONV_A - 1) * SUBLANES
CONV_CHUNK = 32
WRAP_B = 256
WRAP_A = 16
VMEM_LIMIT = 58 * 1024 * 1024

BF16 = jnp.bfloat16
F32 = jnp.float32


def _params(*sem):
    return pltpu.CompilerParams(dimension_semantics=sem, vmem_limit_bytes=VMEM_LIMIT)


def _rms(x, g):
    return x * lax.rsqrt(jnp.mean(x * x, axis=-1, keepdims=True) + RMS_EPS) * g


def _prev_segment(rows):
    moved = pltpu.roll(rows, shift=1, axis=0)
    sub = lax.broadcasted_iota(jnp.int32, rows.shape, 0)
    return jnp.where(sub == 0, 0.0, moved)


def _in_proj_b_kernel(x_ref, g_ref, wga, wgb, wta, wtb, u_o, ga_o, gb_o, h_o):
    @pl.when(pl.program_id(1) == 0)
    def _():
        h_o[...] = _rms(x_ref[...], g_ref[...]).astype(BF16)

    h = h_o[...]

    def dot(w):
        return jnp.dot(h, w[...].astype(BF16), preferred_element_type=F32)

    ga_o[...] = jax.nn.sigmoid(dot(wta)).astype(BF16)
    gb_o[...] = jax.nn.sigmoid(dot(wtb)).astype(BF16)
    glu_gate = jax.nn.sigmoid(dot(wgb))
    u_o[...] = dot(wga) * glu_gate


def _in_proj_b(x, g, w_in, layer, *, tm, tn):
    t = x.shape[0]
    nb = D // tn
    w_specs = [pl.BlockSpec((None, D, tn), functools.partial(lambda m, n, grp: (layer, 0, grp * nb + n), grp=grp))
               for grp in (3, 4, 5, 6)]
    tile = pl.BlockSpec((tm, tn), lambda m, n: (m, n))
    return pl.pallas_call(
        _in_proj_b_kernel,
        grid=(t // tm, nb),
        in_specs=[pl.BlockSpec((tm, D), lambda m, n: (m, 0)),
                  pl.BlockSpec((1, D), lambda m, n: (0, 0))] + w_specs,
        out_specs=[tile] * 3 + [pl.BlockSpec((tm, D), lambda m, n: (m, 0))],
        out_shape=[jax.ShapeDtypeStruct((t, D), F32), jax.ShapeDtypeStruct((t, D), BF16),
                   jax.ShapeDtypeStruct((t, D), BF16), jax.ShapeDtypeStruct((t, D), BF16)],
        compiler_params=_params("arbitrary", "arbitrary"),
        name="in_proj_b",
    )(x, g, w_in, w_in, w_in, w_in)


def _in_proj_a_kernel(h_ref, wb, wc, wv, u_ref, hist_ref, sb_ref, wb8_ref, bias_ref,
                      ba_o, cv_o, cb_o, nsb_o, h_scr, xh_scr,
                      *, tm, n_prompt_tiles, tiles_per_seq, n_batch):
    m = pl.program_id(0)
    groups = CONV_CHUNK // SUBLANES
    n_chunks = tm // CONV_CHUNK

    @pl.when(pl.program_id(1) == 0)
    def _():
        h_scr[...] = h_ref[...]

    def interleaved(conv_chunk):
        n_slices = tm // MXU_SLICE_ROWS
        weights = (wb, wc, wv)
        pieces = [(s, k) for s in range(n_slices) for k in range(len(weights))]
        bounds = [round(p * n_chunks / len(pieces)) for p in range(len(pieces) + 1)]
        for p, (s, k) in enumerate(pieces):
            for ci in range(bounds[p], bounds[p + 1]):
                conv_chunk(ci)
            rows = pl.ds(s * MXU_SLICE_ROWS, MXU_SLICE_ROWS)
            y = jnp.dot(h_scr[rows, :], weights[k][...], preferred_element_type=F32)
            if k == 0:
                ba_o[rows, :] = y.astype(BF16)
            elif k == 1:
                cv_o[rows, :] = y
            else:
                cv_o[rows, :] = cv_o[rows, :] * y

    @pl.when(m < n_prompt_tiles)
    def _prompt():
        first = m % tiles_per_seq == 0
        w0 = hist_ref.shape[0] - HIST_B
        for j in range(CONV_B - 1):
            rows = hist_ref[pl.ds(w0 + j * SUBLANES, SUBLANES), :]
            xh_scr[pl.ds(j * SUBLANES, SUBLANES), :] = jnp.where(first, _prev_segment(rows), rows)
        bias8 = jnp.broadcast_to(bias_ref[...], (SUBLANES, bias_ref.shape[1]))

        def conv_chunk(ci):
            c0 = ci * CONV_CHUNK
            accs = [bias8] * groups
            for k in range(CONV_B):
                w = wb8_ref[k]
                for gi in range(groups):
                    r = c0 + (gi + k) * SUBLANES - HIST_B
                    if r < 0:
                        src = xh_scr[pl.ds(r + HIST_B, SUBLANES), :]
                    else:
                        src = u_ref[pl.ds(r, SUBLANES), :]
                    accs[gi] = accs[gi] + src * w
            for gi in range(groups):
                cb_o[pl.ds(c0 + gi * SUBLANES, SUBLANES), :] = accs[gi]
            return accs[-1]

        interleaved(conv_chunk)

    @pl.when(m == n_prompt_tiles)
    def _sample():
        n_t = tm // n_batch
        bias8 = jnp.broadcast_to(bias_ref[...], (SUBLANES, bias_ref.shape[1]))

        def conv_chunk(ci):
            t, r0 = divmod(ci * CONV_CHUNK, n_batch)
            accs = [bias8] * groups
            for k in range(CONV_B):
                j = t + k
                w = wb8_ref[k]
                for gi in range(groups):
                    g0 = r0 + gi * SUBLANES
                    if j < CONV_B - 1:
                        src = sb_ref[j, pl.ds(g0, SUBLANES), :]
                    else:
                        src = u_ref[pl.ds((j - (CONV_B - 1)) * n_batch + g0, SUBLANES), :]
                    accs[gi] = accs[gi] + src * w
            for gi in range(groups):
                cb_o[pl.ds(t * n_batch + r0 + gi * SUBLANES, SUBLANES), :] = accs[gi]
            return accs[-1]

        interleaved(conv_chunk)
        for j in range(CONV_B - 1):
            jj = j + n_t
            if jj < CONV_B - 1:
                nsb_o[j] = sb_ref[jj]
            else:
                nsb_o[j] = u_ref[pl.ds((jj - (CONV_B - 1)) * n_batch, n_batch), :]


def _in_proj_a(h, w_a, layer, u, state_b_tm, conv_b_w, conv_b_bias, *, tm, tn, n_seq, seq, n_batch):
    t = h.shape[0]
    nb = D // tn
    n_prompt_tiles = (n_seq * seq) // tm
    tiles_per_seq = seq // tm
    assert t - n_seq * seq == tm and tm % n_batch == 0 and seq % tm == 0 and tm % WRAP_B == 0
    per_tile = tm // WRAP_B

    def hist_index(m, n):
        seq_last = (jnp.minimum(m // tiles_per_seq, n_seq - 1) + 1) * (seq // WRAP_B) - 1
        return jnp.where(m % tiles_per_seq == 0, seq_last, m * per_tile - 1), n

    def sample_only(m, n):
        return jnp.where(m == n_prompt_tiles, n, 0)

    w_specs = [pl.BlockSpec((None, D, tn), lambda m, n: (layer, 0, n))] * len(w_a)
    tile = pl.BlockSpec((tm, tn), lambda m, n: (m, n))
    state = pl.BlockSpec((CONV_B - 1, n_batch, tn), lambda m, n: (0, 0, sample_only(m, n)))
    kern = functools.partial(_in_proj_a_kernel, tm=tm, n_prompt_tiles=n_prompt_tiles,
                             tiles_per_seq=tiles_per_seq, n_batch=n_batch)
    return pl.pallas_call(
        kern,
        grid=(t // tm, nb),
        in_specs=[pl.BlockSpec((tm, D), lambda m, n: (m, 0))] + w_specs + [
                  tile,
                  pl.BlockSpec((WRAP_B, tn), hist_index),
                  state,
                  pl.BlockSpec((CONV_B, SUBLANES, tn), lambda m, n: (0, 0, n)),
                  pl.BlockSpec((1, tn), lambda m, n: (0, n))],
        out_specs=[tile, tile, tile, state],
        out_shape=[jax.ShapeDtypeStruct((t, D), BF16), jax.ShapeDtypeStruct((t, D), F32),
                   jax.ShapeDtypeStruct((t, D), F32),
                   jax.ShapeDtypeStruct((CONV_B - 1, n_batch, D), F32)],
        scratch_shapes=[pltpu.VMEM((tm, D), BF16), pltpu.VMEM((HIST_B, tn), F32)],
        compiler_params=_params("arbitrary", "arbitrary"),
        name="in_proj_a",
    )(h, *w_a, u, u, state_b_tm,
      jnp.broadcast_to(conv_b_w[:, None, :], (CONV_B, SUBLANES, D)), conv_b_bias)


def _conv_a_kernel(cv_ref, ba_ref, cw_ref, uw_ref, sa_ref, wa_ref, ya_o, pa_o, pb_o, nsa_o, xc_scr,
                   *, tc, n_prompt_tiles, tiles_per_seq, n_batch):
    i = pl.program_id(1)

    def taps(src_of_tap, rows):
        acc = None
        for k in range(CONV_A):
            term = src_of_tap(k) * wa_ref[k:k + 1, :]
            acc = term if acc is None else acc + term
        ya_o[rows, :] = (ba_ref[rows, :].astype(F32) * acc).astype(BF16)

    @pl.when(i < n_prompt_tiles)
    def _prompt():
        @pl.when(i % tiles_per_seq == 0)
        def _():
            w0 = cw_ref.shape[0] - HIST_A
            for j in range(CONV_A - 1):
                xc_scr[pl.ds(j * SUBLANES, SUBLANES), :] = _prev_segment(
                    cw_ref[pl.ds(w0 + j * SUBLANES, SUBLANES), :])

        @pl.when(i % tiles_per_seq != 0)
        def _():
            xc_scr[0:HIST_A, :] = xc_scr[tc:tc + HIST_A, :]

        xc_scr[HIST_A:HIST_A + tc, :] = cv_ref[...]

        def chunk(ci, carry):
            r0 = pl.multiple_of(ci * CONV_CHUNK, CONV_CHUNK)
            taps(lambda k: xc_scr[pl.ds(r0 + k * SUBLANES, CONV_CHUNK), :], pl.ds(r0, CONV_CHUNK))
            return carry

        lax.fori_loop(0, tc // CONV_CHUNK, chunk, 0)

        @pl.when(i % tiles_per_seq == tiles_per_seq - 1)
        def _():
            last = SUBLANES - 1
            for j in range(CONV_A - 1):
                pa_o[pl.ds(j, 1), :] = cv_ref[pl.ds(tc - HIST_A + j * SUBLANES + last, 1), :]
            for j in range(CONV_B - 1):
                pb_o[pl.ds(j, 1), :] = uw_ref[pl.ds(uw_ref.shape[0] - HIST_B + j * SUBLANES + last, 1), :]

    @pl.when(i == n_prompt_tiles)
    def _sample():
        n_t = tc // n_batch
        for t in range(n_t):
            def src_of_tap(k, t=t):
                j = t + k
                if j < CONV_A - 1:
                    return sa_ref[j]
                return cv_ref[pl.ds((j - (CONV_A - 1)) * n_batch, n_batch), :]

            taps(src_of_tap, pl.ds(t * n_batch, n_batch))
        for j in range(CONV_A - 1):
            jj = j + n_t
            if jj < CONV_A - 1:
                nsa_o[j] = sa_ref[jj]
            else:
                nsa_o[j] = cv_ref[pl.ds((jj - (CONV_A - 1)) * n_batch, n_batch), :]


def _conv_a(cv, ba, u, state_a_tm, conv_a_w, *, tc, cb, n_seq, seq, n_batch):
    t = cv.shape[0]
    n_prompt_tiles = (n_seq * seq) // tc
    tiles_per_seq = seq // tc
    assert t - n_seq * seq == tc and tc % n_batch == 0 and seq % tc == 0 and tc % CONV_CHUNK == 0

    def seq_of(i):
        return jnp.minimum(i // tiles_per_seq, n_seq - 1)

    tile = pl.BlockSpec((tc, cb), lambda c, i: (i, c))
    state = pl.BlockSpec((CONV_A - 1, n_batch, cb), lambda c, i: (0, 0, c))
    kern = functools.partial(_conv_a_kernel, tc=tc, n_prompt_tiles=n_prompt_tiles, tiles_per_seq=tiles_per_seq,
                             n_batch=n_batch)
    return pl.pallas_call(
        kern,
        grid=(D // cb, t // tc),
        in_specs=[tile, tile,
                  pl.BlockSpec((WRAP_A, cb), lambda c, i: ((seq_of(i) + 1) * (seq // WRAP_A) - 1, c)),
                  pl.BlockSpec((WRAP_B, cb), lambda c, i: ((seq_of(i) + 1) * (seq // WRAP_B) - 1, c)),
                  state,
                  pl.BlockSpec((CONV_A, cb), lambda c, i: (0, c))],
        out_specs=[tile,
                   pl.BlockSpec((None, CONV_A - 1, cb), lambda c, i: (seq_of(i), 0, c)),
                   pl.BlockSpec((None, CONV_B - 1, cb), lambda c, i: (seq_of(i), 0, c)),
                   state],
        out_shape=[jax.ShapeDtypeStruct((t, D), BF16),
                   jax.ShapeDtypeStruct((n_seq, CONV_A - 1, D), F32),
                   jax.ShapeDtypeStruct((n_seq, CONV_B - 1, D), F32),
                   jax.ShapeDtypeStruct((CONV_A - 1, n_batch, D), F32)],
        scratch_shapes=[pltpu.VMEM((HIST_A + tc, cb), F32)],
        compiler_params=_params("arbitrary", "arbitrary"),
        name="conv_a",
    )(cv, ba, cv, u, state_a_tm, conv_a_w)


def _mix_out_kernel(cb_ref, lg_ref, lb_ref, ya_ref, ga_ref, gb_ref, woa_ref, wob_ref, wo_ref, x_ref, o_ref, z_scr):
    c = cb_ref[...]
    mu = jnp.mean(c, axis=-1, keepdims=True)
    xc = c - mu
    var = jnp.mean(xc * xc, axis=-1, keepdims=True)
    y = xc * lax.rsqrt(var + LN_EPS) * lg_ref[...] + lb_ref[...]
    z_scr[...] = (y * jax.nn.sigmoid(y)).astype(BF16)

    y_a = jnp.dot(ya_ref[...], woa_ref[...], preferred_element_type=F32)
    y_b = jnp.dot(z_scr[...], wob_ref[...], preferred_element_type=F32)
    merged = (ga_ref[...].astype(F32) * y_a + gb_ref[...].astype(F32) * y_b).astype(BF16)
    o_ref[...] = x_ref[...] + jnp.dot(merged, wo_ref[...], preferred_element_type=F32)


def _mix_out(conv_b, ln_g, ln_b, ya_in, ga, gb, w_out_a, w_out_b, w_o, layer, x, *, tm):
    t = conv_b.shape[0]
    row = pl.BlockSpec((tm, D), lambda m: (m, 0))
    vec = pl.BlockSpec((1, D), lambda m: (0, 0))
    wfull = pl.BlockSpec((None, D, D), lambda m: (layer, 0, 0), pipeline_mode=pl.Buffered(1))
    return pl.pallas_call(
        _mix_out_kernel,
        grid=(t // tm,),
        in_specs=[row, vec, vec, row, row, row, wfull, wfull, wfull, row],
        out_specs=row,
        out_shape=jax.ShapeDtypeStruct((t, D), F32),
        scratch_shapes=[pltpu.VMEM((tm, D), BF16)],
        compiler_params=_params("arbitrary"),
        name="mix_out",
    )(conv_b, ln_g, ln_b, ya_in, ga, gb, w_out_a, w_out_b, w_o, x)


def _ffn_tile(x_ref, g_ref, wg_ref, wu_ref, wd_ref, gf_ref, o_ref, h_scr, final_norm):
    j = pl.program_id(1)

    @pl.when(j == 0)
    def _():
        x = x_ref[...]
        h_scr[...] = _rms(x, g_ref[...]).astype(BF16)
        o_ref[...] = x

    h = h_scr[...]
    gate = jnp.dot(h, wg_ref[...].astype(BF16), preferred_element_type=F32)
    up = jnp.dot(h, wu_ref[...].astype(BF16), preferred_element_type=F32)
    act = (gate * jax.nn.sigmoid(gate) * up).astype(BF16)
    o_ref[...] += jnp.dot(act, wd_ref[...].astype(BF16), preferred_element_type=F32)

    if final_norm:
        @pl.when(j == pl.num_programs(1) - 1)
        def _():
            o_ref[...] = _rms(o_ref[...], gf_ref[...])


def _ffn_kernel(x_ref, g_ref, wg_ref, wu_ref, wd_ref, gf_ref, o_ref, h_scr):
    _ffn_tile(x_ref, g_ref, wg_ref, wu_ref, wd_ref, gf_ref, o_ref, h_scr, False)


def _ffn_final_kernel(x_ref, g_ref, wg_ref, wu_ref, wd_ref, gf_ref, op_ref, os_ref, h_scr, *, n_prompt_tiles):
    m = pl.program_id(0)

    @pl.when(m < n_prompt_tiles)
    def _():
        _ffn_tile(x_ref, g_ref, wg_ref, wu_ref, wd_ref, gf_ref, op_ref, h_scr, True)

    @pl.when(m >= n_prompt_tiles)
    def _():
        _ffn_tile(x_ref, g_ref, wg_ref, wu_ref, wd_ref, gf_ref, os_ref, h_scr, True)


def _ffn(x, g, w_gate, w_up, w_down, layer, g_final, *, tm, tf, n_prompt=None):
    t = x.shape[0]
    vec = pl.BlockSpec((1, D), lambda m, j: (0, 0))
    if n_prompt is None:
        body = _ffn_kernel
        out_specs = pl.BlockSpec((tm, D), lambda m, j: (m, 0))
        out_shape = jax.ShapeDtypeStruct((t, D), F32)
    else:
        npt = n_prompt // tm
        body = functools.partial(_ffn_final_kernel, n_prompt_tiles=npt)
        out_specs = [pl.BlockSpec((tm, D), lambda m, j: (jnp.minimum(m, npt - 1), 0)),
                     pl.BlockSpec((tm, D), lambda m, j: (jnp.maximum(m - npt, 0), 0),
                                  pipeline_mode=pl.Buffered(1))]
        out_shape = [jax.ShapeDtypeStruct((n_prompt, D), F32), jax.ShapeDtypeStruct((t - n_prompt, D), F32)]
    return pl.pallas_call(
        body,
        grid=(t // tm, D_FF // tf),
        in_specs=[pl.BlockSpec((tm, D), lambda m, j: (m, 0)), vec,
                  pl.BlockSpec((None, D, tf), lambda m, j: (layer, 0, j)),
                  pl.BlockSpec((None, D, tf), lambda m, j: (layer, 0, j)),
                  pl.BlockSpec((None, tf, D), lambda m, j: (layer, j, 0)),
                  vec],
        out_specs=out_specs,
        out_shape=out_shape,
        scratch_shapes=[pltpu.VMEM((tm, D), BF16)],
        compiler_params=_params("arbitrary", "arbitrary"),
        name="ffn",
    )(x, g, w_gate, w_up, w_down, g_final)


def _cast_kernel(*refs):
    n = len(refs) // 2
    for w_ref, o_ref in zip(refs[:n], refs[n:]):
        o_ref[...] = w_ref[...].astype(BF16)


def _cast_groups(w, n_groups, *, rows):
    depth, n_rows, _ = w.shape
    return pl.pallas_call(
        _cast_kernel,
        grid=(depth, n_rows // rows),
        in_specs=[pl.BlockSpec((None, rows, D), functools.partial(lambda l, r, grp: (l, r, grp), grp=grp))
                  for grp in range(n_groups)],
        out_specs=[pl.BlockSpec((None, rows, D), lambda l, r: (l, r, 0))] * n_groups,
        out_shape=[jax.ShapeDtypeStruct((depth, n_rows, D), BF16)] * n_groups,
        compiler_params=_params("arbitrary", "arbitrary"),
        name="cast_groups",
    )(*([w] * n_groups))


def kernel(x_prompt, x_sample, state_conv_a, state_conv_b, norm_mix_g, w_in, conv_a_w, w_out_a, conv_b_w,
           conv_b_bias, ln_b_g, ln_b_b, w_out_b, w_o, norm_ffn_g, w_gate, w_up, w_down, final_norm_g):
    n_seq, seq, _ = x_prompt.shape
    n_batch, n_dec, _ = x_sample.shape
    depth = w_in.shape[0]
    n_prompt = n_seq * seq
    n_sample = n_batch * n_dec
    seg = seq // SUBLANES
    shape = dict(n_seq=n_seq, seq=seq, n_batch=n_batch)

    x = jnp.concatenate(
        [x_prompt.reshape(n_seq, SUBLANES, seg, D).transpose(0, 2, 1, 3).reshape(n_prompt, D),
         x_sample.transpose(1, 0, 2).reshape(n_sample, D)], axis=0)
    w_a = _cast_groups(w_in, 3, rows=128)
    w_out_a, w_out_b, w_o = (w.astype(BF16) for w in (w_out_a, w_out_b, w_o))
    g_final = final_norm_g.reshape(1, D)
    pa, pb, sa, sb = [], [], [], []
    for l in range(depth):
        vec = lambda v: v[l].reshape(1, D)
        u, ga, gb, h = _in_proj_b(x, vec(norm_mix_g), w_in, l, tm=n_sample, tn=256)
        ba, cv, conv_b, new_sb = _in_proj_a(
            h, w_a, l, u, state_conv_b[l].transpose(1, 0, 2), conv_b_w[l], vec(conv_b_bias),
            tm=n_sample, tn=256, **shape)
        ya_in, new_pa, new_pb, new_sa = _conv_a(cv, ba, u, state_conv_a[l].transpose(1, 0, 2), conv_a_w[l],
                                                tc=n_sample, cb=512, **shape)
        x = _mix_out(conv_b, vec(ln_b_g), vec(ln_b_b), ya_in, ga, gb, w_out_a, w_out_b, w_o, l, x, tm=256)
        x = _ffn(x, vec(norm_ffn_g), w_gate, w_up, w_down, l, g_final,
                 tm=1024, tf=256, n_prompt=(n_prompt if l == depth - 1 else None))
        pa.append(new_pa)
        pb.append(new_pb)
        sa.append(new_sa.transpose(1, 0, 2))
        sb.append(new_sb.transpose(1, 0, 2))

    y_p, y_s = x
    y_prompt = y_p.reshape(n_seq, seg, SUBLANES, D).transpose(0, 2, 1, 3).reshape(n_seq, seq, D)
    y_sample = y_s.reshape(n_dec, n_batch, D).transpose(1, 0, 2)
    return (y_prompt, y_sample, jnp.stack(pa), jnp.stack(pb), jnp.stack(sa), jnp.stack(sb))
```

```python
import functools

import jax
import jax.numpy as jnp
from jax import lax
from jax.experimental import pallas as pl
from jax.experimental.pallas import tpu as pltpu

D = 2048
D_FF = 5632
CONV_A = 3
CONV_B = 31
RMS_EPS = 1e-6
LN_EPS = 1e-5
SUBLANES = 8
MXU_SLICE_ROWS = 1024
HIST_B = (CONV_B - 1) * SUBLANES
HIST_A = (CONV_A - 1) * SUBLANES
CONV_CHUNK = 32
WRAP_B = 256
WRAP_A = 16
VMEM_LIMIT = 58 * 1024 * 1024

BF16 = jnp.bfloat16
F32 = jnp.float32


def _params(*sem):
    return pltpu.CompilerParams(dimension_semantics=sem, vmem_limit_bytes=VMEM_LIMIT)


def _rms(x, g):
    return x * lax.rsqrt(jnp.mean(x * x, axis=-1, keepdims=True) + RMS_EPS) * g


def _prev_segment(rows):
    moved = pltpu.roll(rows, shift=1, axis=0)
    sub = lax.broadcasted_iota(jnp.int32, rows.shape, 0)
    return jnp.where(sub == 0, 0.0, moved)


def _in_proj_b_kernel(x_ref, g_ref, wga, wgb, wta, wtb, u_o, ga_o, gb_o, h_o):
    @pl.when(pl.program_id(1) == 0)
    def _():
        h_o[...] = _rms(x_ref[...], g_ref[...]).astype(BF16)

    h = h_o[...]

    def dot(w):
        return jnp.dot(h, w[...].astype(BF16), preferred_element_type=F32)

    ga_o[...] = jax.nn.sigmoid(dot(wta)).astype(BF16)
    gb_o[...] = jax.nn.sigmoid(dot(wtb)).astype(BF16)
    glu_gate = jax.nn.sigmoid(dot(wgb))
    u_o[...] = dot(wga) * glu_gate


def _in_proj_b(x, g, w_in, layer, *, tm, tn):
    t = x.shape[0]
    nb = D // tn
    w_specs = [pl.BlockSpec((None, D, tn), functools.partial(lambda m, n, grp: (layer, 0, grp * nb + n), grp=grp))
               for grp in (3, 4, 5, 6)]
    tile = pl.BlockSpec((tm, tn), lambda m, n: (m, n))
    return pl.pallas_call(
        _in_proj_b_kernel,
        grid=(t // tm, nb),
        in_specs=[pl.BlockSpec((tm, D), lambda m, n: (m, 0)),
                  pl.BlockSpec((1, D), lambda m, n: (0, 0))] + w_specs,
        out_specs=[tile] * 3 + [pl.BlockSpec((tm, D), lambda m, n: (m, 0))],
        out_shape=[jax.ShapeDtypeStruct((t, D), F32), jax.ShapeDtypeStruct((t, D), BF16),
                   jax.ShapeDtypeStruct((t, D), BF16), jax.ShapeDtypeStruct((t, D), BF16)],
        compiler_params=_params("arbitrary", "arbitrary"),
        name="in_proj_b",
    )(x, g, w_in, w_in, w_in, w_in)


def _in_proj_a_kernel(h_ref, wb, wc, wv, u_ref, hist_ref, sb_ref, wb8_ref, bias_ref,
                      ba_o, cv_o, cb_o, nsb_o, h_scr, xh_scr,
                      *, tm, n_prompt_tiles, tiles_per_seq, n_batch):
    m = pl.program_id(0)
    groups = CONV_CHUNK // SUBLANES
    n_chunks = tm // CONV_CHUNK

    @pl.when(pl.program_id(1) == 0)
    def _():
        h_scr[...] = h_ref[...]

    def interleaved(conv_chunk):
        n_slices = tm // MXU_SLICE_ROWS
        weights = (wb, wc, wv)
        pieces = [(s, k) for s in range(n_slices) for k in range(len(weights))]
        bounds = [round(p * n_chunks / len(pieces)) for p in range(len(pieces) + 1)]
        for p, (s, k) in enumerate(pieces):
            for ci in range(bounds[p], bounds[p + 1]):
                conv_chunk(ci)
            rows = pl.ds(s * MXU_SLICE_ROWS, MXU_SLICE_ROWS)
            y = jnp.dot(h_scr[rows, :], weights[k][...], preferred_element_type=F32)
            if k == 0:
                ba_o[rows, :] = y.astype(BF16)
            elif k == 1:
                cv_o[rows, :] = y
            else:
                cv_o[rows, :] = cv_o[rows, :] * y

    @pl.when(m < n_prompt_tiles)
    def _prompt():
        first = m % tiles_per_seq == 0
        w0 = hist_ref.shape[0] - HIST_B
        for j in range(CONV_B - 1):
            rows = hist_ref[pl.ds(w0 + j * SUBLANES, SUBLANES), :]
            xh_scr[pl.ds(j * SUBLANES, SUBLANES), :] = jnp.where(first, _prev_segment(rows), rows)
        bias8 = jnp.broadcast_to(bias_ref[...], (SUBLANES, bias_ref.shape[1]))

        def conv_chunk(ci):
            c0 = ci * CONV_CHUNK
            accs = [bias8] * groups
            for k in range(CONV_B):
                w = wb8_ref[k]
                for gi in range(groups):
                    r = c0 + (gi + k) * SUBLANES - HIST_B
                    if r < 0:
                        src = xh_scr[pl.ds(r + HIST_B, SUBLANES), :]
                    else:
                        src = u_ref[pl.ds(r, SUBLANES), :]
                    accs[gi] = accs[gi] + src * w
            for gi in range(groups):
                cb_o[pl.ds(c0 + gi * SUBLANES, SUBLANES), :] = accs[gi]
            return accs[-1]

        interleaved(conv_chunk)

    @pl.when(m == n_prompt_tiles)
    def _sample():
        n_t = tm // n_batch
        bias8 = jnp.broadcast_to(bias_ref[...], (SUBLANES, bias_ref.shape[1]))

        def conv_chunk(ci):
            t, r0 = divmod(ci * CONV_CHUNK, n_batch)
            accs = [bias8] * groups
            for k in range(CONV_B):
                j = t + k
                w = wb8_ref[k]
                for gi in range(groups):
                    g0 = r0 + gi * SUBLANES
                    if j < CONV_B - 1:
                        src = sb_ref[j, pl.ds(g0, SUBLANES), :]
                    else:
                        src = u_ref[pl.ds((j - (CONV_B - 1)) * n_batch + g0, SUBLANES), :]
                    accs[gi] = accs[gi] + src * w
            for gi in range(groups):
                cb_o[pl.ds(t * n_batch + r0 + gi * SUBLANES, SUBLANES), :] = accs[gi]
            return accs[-1]

        interleaved(conv_chunk)
        for j in range(CONV_B - 1):
            jj = j + n_t
            if jj < CONV_B - 1:
                nsb_o[j] = sb_ref[jj]
            else:
                nsb_o[j] = u_ref[pl.ds((jj - (CONV_B - 1)) * n_batch, n_batch), :]


def _in_proj_a(h, w_a, layer, u, state_b_tm, conv_b_w, conv_b_bias, *, tm, tn, n_seq, seq, n_batch):
    t = h.shape[0]
    nb = D // tn
    n_prompt_tiles = (n_seq * seq) // tm
    tiles_per_seq = seq // tm
    assert t - n_seq * seq == tm and tm % n_batch == 0 and seq % tm == 0 and tm % WRAP_B == 0
    per_tile = tm // WRAP_B

    def hist_index(m, n):
        seq_last = (jnp.minimum(m // tiles_per_seq, n_seq - 1) + 1) * (seq // WRAP_B) - 1
        return jnp.where(m % tiles_per_seq == 0, seq_last, m * per_tile - 1), n

    def sample_only(m, n):
        return jnp.where(m == n_prompt_tiles, n, 0)

    w_specs = [pl.BlockSpec((None, D, tn), lambda m, n: (layer, 0, n))] * len(w_a)
    tile = pl.BlockSpec((tm, tn), lambda m, n: (m, n))
    state = pl.BlockSpec((CONV_B - 1, n_batch, tn), lambda m, n: (0, 0, sample_only(m, n)))
    kern = functools.partial(_in_proj_a_kernel, tm=tm, n_prompt_tiles=n_prompt_tiles,
                             tiles_per_seq=tiles_per_seq, n_batch=n_batch)
    return pl.pallas_call(
        kern,
        grid=(t // tm, nb),
        in_specs=[pl.BlockSpec((tm, D), lambda m, n: (m, 0))] + w_specs + [
                  tile,
                  pl.BlockSpec((WRAP_B, tn), hist_index),
                  state,
                  pl.BlockSpec((CONV_B, SUBLANES, tn), lambda m, n: (0, 0, n)),
                  pl.BlockSpec((1, tn), lambda m, n: (0, n))],
        out_specs=[tile, tile, tile, state],
        out_shape=[jax.ShapeDtypeStruct((t, D), BF16), jax.ShapeDtypeStruct((t, D), F32),
                   jax.ShapeDtypeStruct((t, D), F32),
                   jax.ShapeDtypeStruct((CONV_B - 1, n_batch, D), F32)],
        scratch_shapes=[pltpu.VMEM((tm, D), BF16), pltpu.VMEM((HIST_B, tn), F32)],
        compiler_params=_params("arbitrary", "arbitrary"),
        name="in_proj_a",
    )(h, *w_a, u, u, state_b_tm,
      jnp.broadcast_to(conv_b_w[:, None, :], (CONV_B, SUBLANES, D)), conv_b_bias)


def _conv_a_kernel(cv_ref, ba_ref, cw_ref, uw_ref, sa_ref, wa_ref, ya_o, pa_o, pb_o, nsa_o, xc_scr,
                   *, tc, n_prompt_tiles, tiles_per_seq, n_batch):
    i = pl.program_id(1)

    def taps(src_of_tap, rows):
        acc = None
        for k in range(CONV_A):
            term = src_of_tap(k) * wa_ref[k:k + 1, :]
            acc = term if acc is None else acc + term
        ya_o[rows, :] = (ba_ref[rows, :].astype(F32) * acc).astype(BF16)

    @pl.when(i < n_prompt_tiles)
    def _prompt():
        @pl.when(i % tiles_per_seq == 0)
        def _():
            w0 = cw_ref.shape[0] - HIST_A
            for j in range(CONV_A - 1):
                xc_scr[pl.ds(j * SUBLANES, SUBLANES), :] = _prev_segment(
                    cw_ref[pl.ds(w0 + j * SUBLANES, SUBLANES), :])

        @pl.when(i % tiles_per_seq != 0)
        def _():
            xc_scr[0:HIST_A, :] = xc_scr[tc:tc + HIST_A, :]

        xc_scr[HIST_A:HIST_A + tc, :] = cv_ref[...]

        def chunk(ci, carry):
            r0 = pl.multiple_of(ci * CONV_CHUNK, CONV_CHUNK)
            taps(lambda k: xc_scr[pl.ds(r0 + k * SUBLANES, CONV_CHUNK), :], pl.ds(r0, CONV_CHUNK))
            return carry

        lax.fori_loop(0, tc // CONV_CHUNK, chunk, 0)

        @pl.when(i % tiles_per_seq == tiles_per_seq - 1)
        def _():
            last = SUBLANES - 1
            for j in range(CONV_A - 1):
                pa_o[pl.ds(j, 1), :] = cv_ref[pl.ds(tc - HIST_A + j * SUBLANES + last, 1), :]
            for j in range(CONV_B - 1):
                pb_o[pl.ds(j, 1), :] = uw_ref[pl.ds(uw_ref.shape[0] - HIST_B + j * SUBLANES + last, 1), :]

    @pl.when(i == n_prompt_tiles)
    def _sample():
        n_t = tc // n_batch
        for t in range(n_t):
            def src_of_tap(k, t=t):
                j = t + k
                if j < CONV_A - 1:
                    return sa_ref[j]
                return cv_ref[pl.ds((j - (CONV_A - 1)) * n_batch, n_batch), :]

            taps(src_of_tap, pl.ds(t * n_batch, n_batch))
        for j in range(CONV_A - 1):
            jj = j + n_t
            if jj < CONV_A - 1:
                nsa_o[j] = sa_ref[jj]
            else:
                nsa_o[j] = cv_ref[pl.ds((jj - (CONV_A - 1)) * n_batch, n_batch), :]


def _conv_a(cv, ba, u, state_a_tm, conv_a_w, *, tc, cb, n_seq, seq, n_batch):
    t = cv.shape[0]
    n_prompt_tiles = (n_seq * seq) // tc
    tiles_per_seq = seq // tc
    assert t - n_seq * seq == tc and tc % n_batch == 0 and seq % tc == 0 and tc % CONV_CHUNK == 0

    def seq_of(i):
        return jnp.minimum(i // tiles_per_seq, n_seq - 1)

    tile = pl.BlockSpec((tc, cb), lambda c, i: (i, c))
    state = pl.BlockSpec((CONV_A - 1, n_batch, cb), lambda c, i: (0, 0, c))
    kern = functools.partial(_conv_a_kernel, tc=tc, n_prompt_tiles=n_prompt_tiles, tiles_per_seq=tiles_per_seq,
                             n_batch=n_batch)
    return pl.pallas_call(
        kern,
        grid=(D // cb, t // tc),
        in_specs=[tile, tile,
                  pl.BlockSpec((WRAP_A, cb), lambda c, i: ((seq_of(i) + 1) * (seq // WRAP_A) - 1, c)),
                  pl.BlockSpec((WRAP_B, cb), lambda c, i: ((seq_of(i) + 1) * (seq // WRAP_B) - 1, c)),
                  state,
                  pl.BlockSpec((CONV_A, cb), lambda c, i: (0, c))],
        out_specs=[tile,
                   pl.BlockSpec((None, CONV_A - 1, cb), lambda c, i: (seq_of(i), 0, c)),
                   pl.BlockSpec((None, CONV_B - 1, cb), lambda c, i: (seq_of(i), 0, c)),
                   state],
        out_shape=[jax.ShapeDtypeStruct((t, D), BF16),
                   jax.ShapeDtypeStruct((n_seq, CONV_A - 1, D), F32),
                   jax.ShapeDtypeStruct((n_seq, CONV_B - 1, D), F32),
                   jax.ShapeDtypeStruct((CONV_A - 1, n_batch, D), F32)],
        scratch_shapes=[pltpu.VMEM((HIST_A + tc, cb), F32)],
        compiler_params=_params("arbitrary", "arbitrary"),
        name="conv_a",
    )(cv, ba, cv, u, state_a_tm, conv_a_w)


def _mix_out_kernel(cb_ref, lg_ref, lb_ref, ya_ref, ga_ref, gb_ref, woa_ref, wob_ref, wo_ref, x_ref, o_ref, z_scr):
    c = cb_ref[...]
    mu = jnp.mean(c, axis=-1, keepdims=True)
    xc = c - mu
    var = jnp.mean(xc * xc, axis=-1, keepdims=True)
    y = xc * lax.rsqrt(var + LN_EPS) * lg_ref[...] + lb_ref[...]
    z_scr[...] = (y * jax.nn.sigmoid(y)).astype(BF16)

    y_a = jnp.dot(ya_ref[...], woa_ref[...], preferred_element_type=F32)
    y_b = jnp.dot(z_scr[...], wob_ref[...], preferred_element_type=F32)
    merged = (ga_ref[...].astype(F32) * y_a + gb_ref[...].astype(F32) * y_b).astype(BF16)
    o_ref[...] = x_ref[...] + jnp.dot(merged, wo_ref[...], preferred_element_type=F32)


def _mix_out(conv_b, ln_g, ln_b, ya_in, ga, gb, w_out_a, w_out_b, w_o, layer, x, *, tm):
    t = conv_b.shape[0]
    row = pl.BlockSpec((tm, D), lambda m: (m, 0))
    vec = pl.BlockSpec((1, D), lambda m: (0, 0))
    wfull = pl.BlockSpec((None, D, D), lambda m: (layer, 0, 0), pipeline_mode=pl.Buffered(1))
    return pl.pallas_call(
        _mix_out_kernel,
        grid=(t // tm,),
        in_specs=[row, vec, vec, row, row, row, wfull, wfull, wfull, row],
        out_specs=row,
        out_shape=jax.ShapeDtypeStruct((t, D), F32),
        scratch_shapes=[pltpu.VMEM((tm, D), BF16)],
        compiler_params=_params("arbitrary"),
        name="mix_out",
    )(conv_b, ln_g, ln_b, ya_in, ga, gb, w_out_a, w_out_b, w_o, x)


def _ffn_tile(x_ref, g_ref, wg_ref, wu_ref, wd_ref, gf_ref, o_ref, h_scr, final_norm):
    j = pl.program_id(1)

    @pl.when(j == 0)
    def _():
        x = x_ref[...]
        h_scr[...] = _rms(x, g_ref[...]).astype(BF16)
        o_ref[...] = x

    h = h_scr[...]
    gate = jnp.dot(h, wg_ref[...].astype(BF16), preferred_element_type=F32)
    up = jnp.dot(h, wu_ref[...].astype(BF16), preferred_element_type=F32)
    act = (gate * jax.nn.sigmoid(gate) * up).astype(BF16)
    o_ref[...] += jnp.dot(act, wd_ref[...].astype(BF16), preferred_element_type=F32)

    if final_norm:
        @pl.when(j == pl.num_programs(1) - 1)
        def _():
            o_ref[...] = _rms(o_ref[...], gf_ref[...])


def _ffn_kernel(x_ref, g_ref, wg_ref, wu_ref, wd_ref, gf_ref, o_ref, h_scr):
    _ffn_tile(x_ref, g_ref, wg_ref, wu_ref, wd_ref, gf_ref, o_ref, h_scr, False)


def _ffn_final_kernel(x_ref, g_ref, wg_ref, wu_ref, wd_ref, gf_ref, op_ref, os_ref, h_scr, *, n_prompt_tiles):
    m = pl.program_id(0)

    @pl.when(m < n_prompt_tiles)
    def _():
        _ffn_tile(x_ref, g_ref, wg_ref, wu_ref, wd_ref, gf_ref, op_ref, h_scr, True)

    @pl.when(m >= n_prompt_tiles)
    def _():
        _ffn_tile(x_ref, g_ref, wg_ref, wu_ref, wd_ref, gf_ref, os_ref, h_scr, True)


def _ffn(x, g, w_gate, w_up, w_down, layer, g_final, *, tm, tf, n_prompt=None):
    t = x.shape[0]
    vec = pl.BlockSpec((1, D), lambda m, j: (0, 0))
    if n_prompt is None:
        body = _ffn_kernel
        out_specs = pl.BlockSpec((tm, D), lambda m, j: (m, 0))
        out_shape = jax.ShapeDtypeStruct((t, D), F32)
    else:
        npt = n_prompt // tm
        body = functools.partial(_ffn_final_kernel, n_prompt_tiles=npt)
        out_specs = [pl.BlockSpec((tm, D), lambda m, j: (jnp.minimum(m, npt - 1), 0)),
                     pl.BlockSpec((tm, D), lambda m, j: (jnp.maximum(m - npt, 0), 0),
                                  pipeline_mode=pl.Buffered(1))]
        out_shape = [jax.ShapeDtypeStruct((n_prompt, D), F32), jax.ShapeDtypeStruct((t - n_prompt, D), F32)]
    return pl.pallas_call(
        body,
        grid=(t // tm, D_FF // tf),
        in_specs=[pl.BlockSpec((tm, D), lambda m, j: (m, 0)), vec,
                  pl.BlockSpec((None, D, tf), lambda m, j: (layer, 0, j)),
                  pl.BlockSpec((None, D, tf), lambda m, j: (layer, 0, j)),
                  pl.BlockSpec((None, tf, D), lambda m, j: (layer, j, 0)),
                  vec],
        out_specs=out_specs,
        out_shape=out_shape,
        scratch_shapes=[pltpu.VMEM((tm, D), BF16)],
        compiler_params=_params("arbitrary", "arbitrary"),
        name="ffn",
    )(x, g, w_gate, w_up, w_down, g_final)


def _cast_kernel(*refs):
    n = len(refs) // 2
    for w_ref, o_ref in zip(refs[:n], refs[n:]):
        o_ref[...] = w_ref[...].astype(BF16)


def _cast_groups(w, n_groups, *, rows):
    depth, n_rows, _ = w.shape
    return pl.pallas_call(
        _cast_kernel,
        grid=(depth, n_rows // rows),
        in_specs=[pl.BlockSpec((None, rows, D), functools.partial(lambda l, r, grp: (l, r, grp), grp=grp))
                  for grp in range(n_groups)],
        out_specs=[pl.BlockSpec((None, rows, D), lambda l, r: (l, r, 0))] * n_groups,
        out_shape=[jax.ShapeDtypeStruct((depth, n_rows, D), BF16)] * n_groups,
        compiler_params=_params("arbitrary", "arbitrary"),
        name="cast_groups",
    )(*([w] * n_groups))


def kernel(x_prompt, x_sample, state_conv_a, state_conv_b, norm_mix_g, w_in, conv_a_w, w_out_a, conv_b_w,
           conv_b_bias, ln_b_g, ln_b_b, w_out_b, w_o, norm_ffn_g, w_gate, w_up, w_down, final_norm_g):
    n_seq, seq, _ = x_prompt.shape
    n_batch, n_dec, _ = x_sample.shape
    depth = w_in.shape[0]
    n_prompt = n_seq * seq
    n_sample = n_batch * n_dec
    seg = seq // SUBLANES
    shape = dict(n_seq=n_seq, seq=seq, n_batch=n_batch)

    x = jnp.concatenate(
        [x_prompt.reshape(n_seq, SUBLANES, seg, D).transpose(0, 2, 1, 3).reshape(n_prompt, D),
         x_sample.transpose(1, 0, 2).reshape(n_sample, D)], axis=0)
    w_a = _cast_groups(w_in, 3, rows=128)
    w_out_a, w_out_b, w_o = (w.astype(BF16) for w in (w_out_a, w_out_b, w_o))
    g_final = final_norm_g.reshape(1, D)
    pa, pb, sa, sb = [], [], [], []
    for l in range(depth):
        vec = lambda v: v[l].reshape(1, D)
        u, ga, gb, h = _in_proj_b(x, vec(norm_mix_g), w_in, l, tm=n_sample, tn=256)
        ba, cv, conv_b, new_sb = _in_proj_a(
            h, w_a, l, u, state_conv_b[l].transpose(1, 0, 2), conv_b_w[l], vec(conv_b_bias),
            tm=n_sample, tn=256, **shape)
        ya_in, new_pa, new_pb, new_sa = _conv_a(cv, ba, u, state_conv_a[l].transpose(1, 0, 2), conv_a_w[l],
                                                tc=n_sample, cb=512, **shape)
        x = _mix_out(conv_b, vec(ln_b_g), vec(ln_b_b), ya_in, ga, gb, w_out_a, w_out_b, w_o, l, x, tm=256)
        x = _ffn(x, vec(norm_ffn_g), w_gate, w_up, w_down, l, g_final,
                 tm=1024, tf=256, n_prompt=(n_prompt if l == depth - 1 else None))
        pa.append(new_pa)
        pb.append(new_pb)
        sa.append(new_sa.transpose(1, 0, 2))
        sb.append(new_sb.transpose(1, 0, 2))

    y_p, y_s = x
    y_prompt = y_p.reshape(n_seq, seg, SUBLANES, D).transpose(0, 2, 1, 3).reshape(n_seq, seq, D)
    y_sample = y_s.reshape(n_dec, n_batch, D).transpose(1, 0, 2)
    return (y_prompt, y_sample, jnp.stack(pa), jnp.stack(pb), jnp.stack(sa), jnp.stack(sb))
```

```python
import functools

import jax
import jax.numpy as jnp
from jax import lax
from jax.experimental import pallas as pl
from jax.experimental.pallas import tpu as pltpu

D = 2048
D_FF = 5632
CONV_A = 3
CONV_B = 31
RMS_EPS = 1e-6
LN_EPS = 1e-5
SUBLANES = 8
MXU_SLICE_ROWS = 512
HIST_B = (CONV_B - 1) * SUBLANES
HIST_A = (CONV_A - 1) * SUBLANES
CONV_CHUNK = 32
WRAP_B = 256
WRAP_A = 16
VMEM_LIMIT = 58 * 1024 * 1024

BF16 = jnp.bfloat16
F32 = jnp.float32


def _params(*sem):
    return pltpu.CompilerParams(dimension_semantics=sem, vmem_limit_bytes=VMEM_LIMIT)


def _rms(x, g):
    return x * lax.rsqrt(jnp.mean(x * x, axis=-1, keepdims=True) + RMS_EPS) * g


def _prev_segment(rows):
    moved = pltpu.roll(rows, shift=1, axis=0)
    sub = lax.broadcasted_iota(jnp.int32, rows.shape, 0)
    return jnp.where(sub == 0, 0.0, moved)


def _in_proj_b_kernel(x_ref, g_ref, wga, wgb, wta, wtb, u_o, ga_o, gb_o, h_o):
    @pl.when(pl.program_id(1) == 0)
    def _():
        h_o[...] = _rms(x_ref[...], g_ref[...]).astype(BF16)

    h = h_o[...]

    def dot(w):
        return jnp.dot(h, w[...], preferred_element_type=F32)

    ga_o[...] = jax.nn.sigmoid(dot(wta)).astype(BF16)
    gb_o[...] = jax.nn.sigmoid(dot(wtb)).astype(BF16)
    glu_gate = jax.nn.sigmoid(dot(wgb))
    u_o[...] = dot(wga) * glu_gate


def _in_proj_b(x, g, w_in, layer, *, tm, tn):
    t = x.shape[0]
    nb = D // tn
    w_specs = [pl.BlockSpec((None, D, tn), functools.partial(lambda m, n, grp: (layer, 0, grp * nb + n), grp=grp))
               for grp in (3, 4, 5, 6)]
    tile = pl.BlockSpec((tm, tn), lambda m, n: (m, n))
    return pl.pallas_call(
        _in_proj_b_kernel,
        grid=(t // tm, nb),
        in_specs=[pl.BlockSpec((tm, D), lambda m, n: (m, 0)),
                  pl.BlockSpec((1, D), lambda m, n: (0, 0))] + w_specs,
        out_specs=[tile] * 3 + [pl.BlockSpec((tm, D), lambda m, n: (m, 0))],
        out_shape=[jax.ShapeDtypeStruct((t, D), F32), jax.ShapeDtypeStruct((t, D), BF16),
                   jax.ShapeDtypeStruct((t, D), BF16), jax.ShapeDtypeStruct((t, D), BF16)],
        compiler_params=_params("arbitrary", "arbitrary"),
        name="in_proj_b",
    )(x, g, w_in, w_in, w_in, w_in)


def _in_proj_a_kernel(h_ref, wb, wc, wv, u_ref, hist_ref, sb_ref, wb8_ref, bias_ref,
                      ba_o, cv_o, cb_o, nsb_o, h_scr, xh_scr,
                      *, tm, n_prompt_tiles, tiles_per_seq, n_batch):
    m = pl.program_id(0)
    groups = CONV_CHUNK // SUBLANES
    n_chunks = tm // CONV_CHUNK

    @pl.when(pl.program_id(1) == 0)
    def _():
        h_scr[...] = h_ref[...]

    def interleaved(conv_chunk):
        n_slices = tm // MXU_SLICE_ROWS
        weights = (wb, wc, wv)
        pieces = [(s, k) for s in range(n_slices) for k in range(len(weights))]
        bounds = [round(p * n_chunks / len(pieces)) for p in range(len(pieces) + 1)]
        for p, (s, k) in enumerate(pieces):
            for ci in range(bounds[p], bounds[p + 1]):
                conv_chunk(ci)
            rows = pl.ds(s * MXU_SLICE_ROWS, MXU_SLICE_ROWS)
            y = jnp.dot(h_scr[rows, :], weights[k][...], preferred_element_type=F32)
            if k == 0:
                ba_o[rows, :] = y.astype(BF16)
            elif k == 1:
                cv_o[rows, :] = y
            else:
                cv_o[rows, :] = cv_o[rows, :] * y

    @pl.when(m < n_prompt_tiles)
    def _prompt():
        first = m % tiles_per_seq == 0
        w0 = hist_ref.shape[0] - HIST_B
        for j in range(CONV_B - 1):
            rows = hist_ref[pl.ds(w0 + j * SUBLANES, SUBLANES), :]
            xh_scr[pl.ds(j * SUBLANES, SUBLANES), :] = jnp.where(first, _prev_segment(rows), rows)
        bias8 = jnp.broadcast_to(bias_ref[...], (SUBLANES, bias_ref.shape[1]))

        def conv_chunk(ci):
            c0 = ci * CONV_CHUNK
            accs = [bias8] * groups
            for k in range(CONV_B):
                w = wb8_ref[k]
                for gi in range(groups):
                    r = c0 + (gi + k) * SUBLANES - HIST_B
                    if r < 0:
                        src = xh_scr[pl.ds(r + HIST_B, SUBLANES), :]
                    else:
                        src = u_ref[pl.ds(r, SUBLANES), :]
                    accs[gi] = accs[gi] + src * w
            for gi in range(groups):
                cb_o[pl.ds(c0 + gi * SUBLANES, SUBLANES), :] = accs[gi]
            return accs[-1]

        interleaved(conv_chunk)

    @pl.when(m == n_prompt_tiles)
    def _sample():
        n_t = tm // n_batch
        bias8 = jnp.broadcast_to(bias_ref[...], (SUBLANES, bias_ref.shape[1]))

        def conv_chunk(ci):
            t, r0 = divmod(ci * CONV_CHUNK, n_batch)
            accs = [bias8] * groups
            for k in range(CONV_B):
                j = t + k
                w = wb8_ref[k]
                for gi in range(groups):
                    g0 = r0 + gi * SUBLANES
                    if j < CONV_B - 1:
                        src = sb_ref[j, pl.ds(g0, SUBLANES), :]
                    else:
                        src = u_ref[pl.ds((j - (CONV_B - 1)) * n_batch + g0, SUBLANES), :]
                    accs[gi] = accs[gi] + src * w
            for gi in range(groups):
                cb_o[pl.ds(t * n_batch + r0 + gi * SUBLANES, SUBLANES), :] = accs[gi]
            return accs[-1]

        interleaved(conv_chunk)
        for j in range(CONV_B - 1):
            jj = j + n_t
            if jj < CONV_B - 1:
                nsb_o[j] = sb_ref[jj]
            else:
                nsb_o[j] = u_ref[pl.ds((jj - (CONV_B - 1)) * n_batch, n_batch), :]


def _in_proj_a(h, w_a, layer, u, state_b_tm, conv_b_w, conv_b_bias, *, tm, tn, n_seq, seq, n_batch):
    t = h.shape[0]
    nb = D // tn
    n_prompt_tiles = (n_seq * seq) // tm
    tiles_per_seq = seq // tm
    assert t - n_seq * seq == tm and tm % n_batch == 0 and seq % tm == 0 and tm % WRAP_B == 0
    per_tile = tm // WRAP_B

    def hist_index(m, n):
        seq_last = (jnp.minimum(m // tiles_per_seq, n_seq - 1) + 1) * (seq // WRAP_B) - 1
        return jnp.where(m % tiles_per_seq == 0, seq_last, m * per_tile - 1), n

    def sample_only(m, n):
        return jnp.where(m == n_prompt_tiles, n, 0)

    w_specs = [pl.BlockSpec((None, D, tn), functools.partial(lambda m, n, grp: (layer, 0, grp * nb + n), grp=grp))
               for grp in range(3)]
    tile = pl.BlockSpec((tm, tn), lambda m, n: (m, n))
    state = pl.BlockSpec((CONV_B - 1, n_batch, tn), lambda m, n: (0, 0, sample_only(m, n)))
    kern = functools.partial(_in_proj_a_kernel, tm=tm, n_prompt_tiles=n_prompt_tiles,
                             tiles_per_seq=tiles_per_seq, n_batch=n_batch)
    return pl.pallas_call(
        kern,
        grid=(t // tm, nb),
        in_specs=[pl.BlockSpec((tm, D), lambda m, n: (m, 0))] + w_specs + [
                  tile,
                  pl.BlockSpec((WRAP_B, tn), hist_index),
                  state,
                  pl.BlockSpec((CONV_B, SUBLANES, tn), lambda m, n: (0, 0, n)),
                  pl.BlockSpec((1, tn), lambda m, n: (0, n))],
        out_specs=[tile, tile, tile, state],
        out_shape=[jax.ShapeDtypeStruct((t, D), BF16), jax.ShapeDtypeStruct((t, D), F32),
                   jax.ShapeDtypeStruct((t, D), F32),
                   jax.ShapeDtypeStruct((CONV_B - 1, n_batch, D), F32)],
        scratch_shapes=[pltpu.VMEM((tm, D), BF16), pltpu.VMEM((HIST_B, tn), F32)],
        compiler_params=_params("arbitrary", "arbitrary"),
        name="in_proj_a",
    )(h, w_a, w_a, w_a, u, u, state_b_tm,
      jnp.broadcast_to(conv_b_w[:, None, :], (CONV_B, SUBLANES, D)), conv_b_bias)


def _conv_a_kernel(cv_ref, ba_ref, cw_ref, uw_ref, sa_ref, wa_ref, ya_o, pa_o, pb_o, nsa_o, xc_scr,
                   *, tc, n_prompt_tiles, tiles_per_seq, n_batch):
    i = pl.program_id(1)

    def taps(src_of_tap, rows):
        acc = None
        for k in range(CONV_A):
            term = src_of_tap(k) * wa_ref[k:k + 1, :]
            acc = term if acc is None else acc + term
        ya_o[rows, :] = (ba_ref[rows, :].astype(F32) * acc).astype(BF16)

    @pl.when(i < n_prompt_tiles)
    def _prompt():
        @pl.when(i % tiles_per_seq == 0)
        def _():
            w0 = cw_ref.shape[0] - HIST_A
            for j in range(CONV_A - 1):
                xc_scr[pl.ds(j * SUBLANES, SUBLANES), :] = _prev_segment(
                    cw_ref[pl.ds(w0 + j * SUBLANES, SUBLANES), :])

        @pl.when(i % tiles_per_seq != 0)
        def _():
            xc_scr[0:HIST_A, :] = xc_scr[tc:tc + HIST_A, :]

        xc_scr[HIST_A:HIST_A + tc, :] = cv_ref[...]

        def chunk(ci, carry):
            r0 = pl.multiple_of(ci * CONV_CHUNK, CONV_CHUNK)
            taps(lambda k: xc_scr[pl.ds(r0 + k * SUBLANES, CONV_CHUNK), :], pl.ds(r0, CONV_CHUNK))
            return carry

        lax.fori_loop(0, tc // CONV_CHUNK, chunk, 0)

        @pl.when(i % tiles_per_seq == tiles_per_seq - 1)
        def _():
            last = SUBLANES - 1
            for j in range(CONV_A - 1):
                pa_o[pl.ds(j, 1), :] = cv_ref[pl.ds(tc - HIST_A + j * SUBLANES + last, 1), :]
            for j in range(CONV_B - 1):
                pb_o[pl.ds(j, 1), :] = uw_ref[pl.ds(uw_ref.shape[0] - HIST_B + j * SUBLANES + last, 1), :]

    @pl.when(i == n_prompt_tiles)
    def _sample():
        n_t = tc // n_batch
        for t in range(n_t):
            def src_of_tap(k, t=t):
                j = t + k
                if j < CONV_A - 1:
                    return sa_ref[j]
                return cv_ref[pl.ds((j - (CONV_A - 1)) * n_batch, n_batch), :]

            taps(src_of_tap, pl.ds(t * n_batch, n_batch))
        for j in range(CONV_A - 1):
            jj = j + n_t
            if jj < CONV_A - 1:
                nsa_o[j] = sa_ref[jj]
            else:
                nsa_o[j] = cv_ref[pl.ds((jj - (CONV_A - 1)) * n_batch, n_batch), :]


def _conv_a(cv, ba, u, state_a_tm, conv_a_w, *, tc, cb, n_seq, seq, n_batch):
    t = cv.shape[0]
    n_prompt_tiles = (n_seq * seq) // tc
    tiles_per_seq = seq // tc
    assert t - n_seq * seq == tc and tc % n_batch == 0 and seq % tc == 0 and tc % CONV_CHUNK == 0

    def seq_of(i):
        return jnp.minimum(i // tiles_per_seq, n_seq - 1)

    tile = pl.BlockSpec((tc, cb), lambda c, i: (i, c))
    state = pl.BlockSpec((CONV_A - 1, n_batch, cb), lambda c, i: (0, 0, c))
    kern = functools.partial(_conv_a_kernel, tc=tc, n_prompt_tiles=n_prompt_tiles, tiles_per_seq=tiles_per_seq,
                             n_batch=n_batch)
    return pl.pallas_call(
        kern,
        grid=(D // cb, t // tc),
        in_specs=[tile, tile,
                  pl.BlockSpec((WRAP_A, cb), lambda c, i: ((seq_of(i) + 1) * (seq // WRAP_A) - 1, c)),
                  pl.BlockSpec((WRAP_B, cb), lambda c, i: ((seq_of(i) + 1) * (seq // WRAP_B) - 1, c)),
                  state,
                  pl.BlockSpec((CONV_A, cb), lambda c, i: (0, c))],
        out_specs=[tile,
                   pl.BlockSpec((None, CONV_A - 1, cb), lambda c, i: (seq_of(i), 0, c)),
                   pl.BlockSpec((None, CONV_B - 1, cb), lambda c, i: (seq_of(i), 0, c)),
                   state],
        out_shape=[jax.ShapeDtypeStruct((t, D), BF16),
                   jax.ShapeDtypeStruct((n_seq, CONV_A - 1, D), F32),
                   jax.ShapeDtypeStruct((n_seq, CONV_B - 1, D), F32),
                   jax.ShapeDtypeStruct((CONV_A - 1, n_batch, D), F32)],
        scratch_shapes=[pltpu.VMEM((HIST_A + tc, cb), F32)],
        compiler_params=_params("arbitrary", "arbitrary"),
        name="conv_a",
    )(cv, ba, cv, u, state_a_tm, conv_a_w)


def _mix_out_kernel(cb_ref, lg_ref, lb_ref, ya_ref, ga_ref, gb_ref, woa_ref, wob_ref, wo_ref, x_ref, o_ref, z_scr):
    c = cb_ref[...]
    mu = jnp.mean(c, axis=-1, keepdims=True)
    xc = c - mu
    var = jnp.mean(xc * xc, axis=-1, keepdims=True)
    y = xc * lax.rsqrt(var + LN_EPS) * lg_ref[...] + lb_ref[...]
    z_scr[...] = (y * jax.nn.sigmoid(y)).astype(BF16)

    y_a = jnp.dot(ya_ref[...], woa_ref[...], preferred_element_type=F32)
    y_b = jnp.dot(z_scr[...], wob_ref[...], preferred_element_type=F32)
    merged = (ga_ref[...].astype(F32) * y_a + gb_ref[...].astype(F32) * y_b).astype(BF16)
    o_ref[...] = x_ref[...] + jnp.dot(merged, wo_ref[...], preferred_element_type=F32)


def _mix_out(conv_b, ln_g, ln_b, ya_in, ga, gb, w_out_a, w_out_b, w_o, layer, x, *, tm):
    t = conv_b.shape[0]
    row = pl.BlockSpec((tm, D), lambda m: (m, 0))
    vec = pl.BlockSpec((1, D), lambda m: (0, 0))
    wfull = pl.BlockSpec((None, D, D), lambda m: (layer, 0, 0), pipeline_mode=pl.Buffered(1))
    return pl.pallas_call(
        _mix_out_kernel,
        grid=(t // tm,),
        in_specs=[row, vec, vec, row, row, row, wfull, wfull, wfull, row],
        out_specs=row,
        out_shape=jax.ShapeDtypeStruct((t, D), F32),
        scratch_shapes=[pltpu.VMEM((tm, D), BF16)],
        compiler_params=_params("arbitrary"),
        name="mix_out",
    )(conv_b, ln_g, ln_b, ya_in, ga, gb, w_out_a, w_out_b, w_o, x)


def _ffn_tile(x_ref, g_ref, wg_ref, wu_ref, wd_ref, gf_ref, o_ref, h_scr, final_norm):
    j = pl.program_id(1)

    @pl.when(j == 0)
    def _():
        x = x_ref[...]
        h_scr[...] = _rms(x, g_ref[...]).astype(BF16)
        o_ref[...] = x

    h = h_scr[...]
    gate = jnp.dot(h, wg_ref[...].astype(BF16), preferred_element_type=F32)
    up = jnp.dot(h, wu_ref[...].astype(BF16), preferred_element_type=F32)
    act = (gate * jax.nn.sigmoid(gate) * up).astype(BF16)
    o_ref[...] += jnp.dot(act, wd_ref[...].astype(BF16), preferred_element_type=F32)

    if final_norm:
        @pl.when(j == pl.num_programs(1) - 1)
        def _():
            o_ref[...] = _rms(o_ref[...], gf_ref[...])


def _ffn_kernel(x_ref, g_ref, wg_ref, wu_ref, wd_ref, gf_ref, o_ref, h_scr):
    _ffn_tile(x_ref, g_ref, wg_ref, wu_ref, wd_ref, gf_ref, o_ref, h_scr, False)


def _ffn_final_kernel(x_ref, g_ref, wg_ref, wu_ref, wd_ref, gf_ref, op_ref, os_ref, h_scr, *, n_prompt_tiles):
    m = pl.program_id(0)

    @pl.when(m < n_prompt_tiles)
    def _():
        _ffn_tile(x_ref, g_ref, wg_ref, wu_ref, wd_ref, gf_ref, op_ref, h_scr, True)

    @pl.when(m >= n_prompt_tiles)
    def _():
        _ffn_tile(x_ref, g_ref, wg_ref, wu_ref, wd_ref, gf_ref, os_ref, h_scr, True)


def _ffn(x, g, w_gate, w_up, w_down, layer, g_final, *, tm, tf, n_prompt=None):
    t = x.shape[0]
    vec = pl.BlockSpec((1, D), lambda m, j: (0, 0))
    if n_prompt is None:
        body = _ffn_kernel
        out_specs = pl.BlockSpec((tm, D), lambda m, j: (m, 0))
        out_shape = jax.ShapeDtypeStruct((t, D), F32)
    else:
        npt = n_prompt // tm
        body = functools.partial(_ffn_final_kernel, n_prompt_tiles=npt)
        out_specs = [pl.BlockSpec((tm, D), lambda m, j: (jnp.minimum(m, npt - 1), 0)),
                     pl.BlockSpec((tm, D), lambda m, j: (jnp.maximum(m - npt, 0), 0),
                                  pipeline_mode=pl.Buffered(1))]
        out_shape = [jax.ShapeDtypeStruct((n_prompt, D), F32), jax.ShapeDtypeStruct((t - n_prompt, D), F32)]
    return pl.pallas_call(
        body,
        grid=(t // tm, D_FF // tf),
        in_specs=[pl.BlockSpec((tm, D), lambda m, j: (m, 0)), vec,
                  pl.BlockSpec((None, D, tf), lambda m, j: (layer, 0, j)),
                  pl.BlockSpec((None, D, tf), lambda m, j: (layer, 0, j)),
                  pl.BlockSpec((None, tf, D), lambda m, j: (layer, j, 0)),
                  vec],
        out_specs=out_specs,
        out_shape=out_shape,
        scratch_shapes=[pltpu.VMEM((tm, D), BF16)],
        compiler_params=_params("arbitrary", "arbitrary"),
        name="ffn",
    )(x, g, w_gate, w_up, w_down, g_final)


def kernel(x_prompt, x_sample, state_conv_a, state_conv_b, norm_mix_g, w_in, conv_a_w, w_out_a, conv_b_w,
           conv_b_bias, ln_b_g, ln_b_b, w_out_b, w_o, norm_ffn_g, w_gate, w_up, w_down, final_norm_g):
    n_seq, seq, _ = x_prompt.shape
    n_batch, n_dec, _ = x_sample.shape
    depth = w_in.shape[0]
    n_prompt = n_seq * seq
    n_sample = n_batch * n_dec
    seg = seq // SUBLANES
    shape = dict(n_seq=n_seq, seq=seq, n_batch=n_batch)

    x = jnp.concatenate(
        [x_prompt.reshape(n_seq, SUBLANES, seg, D).transpose(0, 2, 1, 3).reshape(n_prompt, D),
         x_sample.transpose(1, 0, 2).reshape(n_sample, D)], axis=0)
    w_in, w_out_a, w_out_b, w_o = (w.astype(BF16) for w in (w_in, w_out_a, w_out_b, w_o))
    g_final = final_norm_g.reshape(1, D)
    pa, pb, sa, sb = [], [], [], []
    for l in range(depth):
        vec = lambda v: v[l].reshape(1, D)
        u, ga, gb, h = _in_proj_b(x, vec(norm_mix_g), w_in, l, tm=n_sample, tn=512)
        ba, cv, conv_b, new_sb = _in_proj_a(
            h, w_in, l, u, state_conv_b[l].transpose(1, 0, 2), conv_b_w[l], vec(conv_b_bias),
            tm=n_sample, tn=256, **shape)
        ya_in, new_pa, new_pb, new_sa = _conv_a(cv, ba, u, state_conv_a[l].transpose(1, 0, 2), conv_a_w[l],
                                                tc=n_sample, cb=512, **shape)
        x = _mix_out(conv_b, vec(ln_b_g), vec(ln_b_b), ya_in, ga, gb, w_out_a, w_out_b, w_o, l, x, tm=256)
        x = _ffn(x, vec(norm_ffn_g), w_gate, w_up, w_down, l, g_final,
                 tm=1024, tf=256, n_prompt=(n_prompt if l == depth - 1 else None))
        pa.append(new_pa)
        pb.append(new_pb)
        sa.append(new_sa.transpose(1, 0, 2))
        sb.append(new_sb.transpose(1, 0, 2))

    y_p, y_s = x
    y_prompt = y_p.reshape(n_seq, seg, SUBLANES, D).transpose(0, 2, 1, 3).reshape(n_seq, seq, D)
    y_sample = y_s.reshape(n_dec, n_batch, D).transpose(1, 0, 2)
    return (y_prompt, y_sample, jnp.stack(pa), jnp.stack(pb), jnp.stack(sa), jnp.stack(sb))
```

```python
import functools

import jax
import jax.numpy as jnp
from jax import lax
from jax.experimental import pallas as pl
from jax.experimental.pallas import tpu as pltpu

D = 2048
D_FF = 5632
CONV_A = 3
CONV_B = 31
RMS_EPS = 1e-6
LN_EPS = 1e-5
SUBLANES = 8
MXU_SLICE_ROWS = 512
HIST_B = (CONV_B - 1) * SUBLANES
HIST_A = (CONV_A - 1) * SUBLANES
CONV_CHUNK = 32
WRAP_B = 256
WRAP_A = 16
VMEM_LIMIT = 58 * 1024 * 1024

BF16 = jnp.bfloat16
F32 = jnp.float32


def _params(*sem):
    return pltpu.CompilerParams(dimension_semantics=sem, vmem_limit_bytes=VMEM_LIMIT)


def _rms(x, g):
    return x * lax.rsqrt(jnp.mean(x * x, axis=-1, keepdims=True) + RMS_EPS) * g


def _prev_segment(rows):
    moved = pltpu.roll(rows, shift=1, axis=0)
    sub = lax.broadcasted_iota(jnp.int32, rows.shape, 0)
    return jnp.where(sub == 0, 0.0, moved)


def _in_proj_b_kernel(x_ref, g_ref, wga, wgb, wta, wtb, u_o, ga_o, gb_o, h_o):
    @pl.when(pl.program_id(1) == 0)
    def _():
        h_o[...] = _rms(x_ref[...], g_ref[...]).astype(BF16)

    h = h_o[...]

    def dot(w):
        return jnp.dot(h, w[...], preferred_element_type=F32)

    ga_o[...] = jax.nn.sigmoid(dot(wta)).astype(BF16)
    gb_o[...] = jax.nn.sigmoid(dot(wtb)).astype(BF16)
    glu_gate = jax.nn.sigmoid(dot(wgb))
    u_o[...] = dot(wga) * glu_gate


def _in_proj_b(x, g, w_in, layer, *, tm, tn):
    t = x.shape[0]
    nb = D // tn
    w_specs = [pl.BlockSpec((None, D, tn), functools.partial(lambda m, n, grp: (layer, 0, grp * nb + n), grp=grp))
               for grp in (3, 4, 5, 6)]
    tile = pl.BlockSpec((tm, tn), lambda m, n: (m, n))
    return pl.pallas_call(
        _in_proj_b_kernel,
        grid=(t // tm, nb),
        in_specs=[pl.BlockSpec((tm, D), lambda m, n: (m, 0)),
                  pl.BlockSpec((1, D), lambda m, n: (0, 0))] + w_specs,
        out_specs=[tile] * 3 + [pl.BlockSpec((tm, D), lambda m, n: (m, 0))],
        out_shape=[jax.ShapeDtypeStruct((t, D), F32), jax.ShapeDtypeStruct((t, D), BF16),
                   jax.ShapeDtypeStruct((t, D), BF16), jax.ShapeDtypeStruct((t, D), BF16)],
        compiler_params=_params("arbitrary", "arbitrary"),
        name="in_proj_b",
    )(x, g, w_in, w_in, w_in, w_in)


def _in_proj_a_kernel(h_ref, wb, wc, wv, u_ref, hist_ref, sb_ref, wb8_ref, bias_ref,
                      ba_o, cv_o, cb_o, nsb_o, h_scr, xh_scr,
                      *, tm, n_prompt_tiles, tiles_per_seq, n_batch):
    m = pl.program_id(0)
    groups = CONV_CHUNK // SUBLANES
    n_chunks = tm // CONV_CHUNK

    @pl.when(pl.program_id(1) == 0)
    def _():
        h_scr[...] = h_ref[...]

    def interleaved(conv_chunk):
        n_slices = tm // MXU_SLICE_ROWS
        weights = (wb, wc, wv)
        pieces = [(s, k) for s in range(n_slices) for k in range(len(weights))]
        bounds = [round(p * n_chunks / len(pieces)) for p in range(len(pieces) + 1)]
        for p, (s, k) in enumerate(pieces):
            for ci in range(bounds[p], bounds[p + 1]):
                conv_chunk(ci)
            rows = pl.ds(s * MXU_SLICE_ROWS, MXU_SLICE_ROWS)
            y = jnp.dot(h_scr[rows, :], weights[k][...], preferred_element_type=F32)
            if k == 0:
                ba_o[rows, :] = y.astype(BF16)
            elif k == 1:
                cv_o[rows, :] = y
            else:
                cv_o[rows, :] = cv_o[rows, :] * y

    @pl.when(m < n_prompt_tiles)
    def _prompt():
        first = m % tiles_per_seq == 0
        w0 = hist_ref.shape[0] - HIST_B
        for j in range(CONV_B - 1):
            rows = hist_ref[pl.ds(w0 + j * SUBLANES, SUBLANES), :]
            xh_scr[pl.ds(j * SUBLANES, SUBLANES), :] = jnp.where(first, _prev_segment(rows), rows)
        bias8 = jnp.broadcast_to(bias_ref[...], (SUBLANES, bias_ref.shape[1]))

        def conv_chunk(ci):
            c0 = ci * CONV_CHUNK
            accs = [bias8] * groups
            for k in range(CONV_B):
                w = wb8_ref[k]
                for gi in range(groups):
                    r = c0 + (gi + k) * SUBLANES - HIST_B
                    if r < 0:
                        src = xh_scr[pl.ds(r + HIST_B, SUBLANES), :]
                    else:
                        src = u_ref[pl.ds(r, SUBLANES), :]
                    accs[gi] = accs[gi] + src * w
            for gi in range(groups):
                cb_o[pl.ds(c0 + gi * SUBLANES, SUBLANES), :] = accs[gi]
            return accs[-1]

        interleaved(conv_chunk)

    @pl.when(m == n_prompt_tiles)
    def _sample():
        n_t = tm // n_batch
        bias8 = jnp.broadcast_to(bias_ref[...], (SUBLANES, bias_ref.shape[1]))

        def conv_chunk(ci):
            t, r0 = divmod(ci * CONV_CHUNK, n_batch)
            accs = [bias8] * groups
            for k in range(CONV_B):
                j = t + k
                w = wb8_ref[k]
                for gi in range(groups):
                    g0 = r0 + gi * SUBLANES
                    if j < CONV_B - 1:
                        src = sb_ref[j, pl.ds(g0, SUBLANES), :]
                    else:
                        src = u_ref[pl.ds((j - (CONV_B - 1)) * n_batch + g0, SUBLANES), :]
                    accs[gi] = accs[gi] + src * w
            for gi in range(groups):
                cb_o[pl.ds(t * n_batch + r0 + gi * SUBLANES, SUBLANES), :] = accs[gi]
            return accs[-1]

        interleaved(conv_chunk)
        for j in range(CONV_B - 1):
            jj = j + n_t
            if jj < CONV_B - 1:
                nsb_o[j] = sb_ref[jj]
            else:
                nsb_o[j] = u_ref[pl.ds((jj - (CONV_B - 1)) * n_batch, n_batch), :]


def _in_proj_a(h, w_a, layer, u, state_b_tm, conv_b_w, conv_b_bias, *, tm, tn, n_seq, seq, n_batch):
    t = h.shape[0]
    nb = D // tn
    n_prompt_tiles = (n_seq * seq) // tm
    tiles_per_seq = seq // tm
    assert t - n_seq * seq == tm and tm % n_batch == 0 and seq % tm == 0 and tm % WRAP_B == 0
    per_tile = tm // WRAP_B

    def hist_index(m, n):
        seq_last = (jnp.minimum(m // tiles_per_seq, n_seq - 1) + 1) * (seq // WRAP_B) - 1
        return jnp.where(m % tiles_per_seq == 0, seq_last, m * per_tile - 1), n

    def sample_only(m, n):
        return jnp.where(m == n_prompt_tiles, n, 0)

    w_specs = [pl.BlockSpec((None, D, tn), functools.partial(lambda m, n, grp: (layer, 0, grp * nb + n), grp=grp))
               for grp in range(3)]
    tile = pl.BlockSpec((tm, tn), lambda m, n: (m, n))
    state = pl.BlockSpec((CONV_B - 1, n_batch, tn), lambda m, n: (0, 0, sample_only(m, n)))
    kern = functools.partial(_in_proj_a_kernel, tm=tm, n_prompt_tiles=n_prompt_tiles,
                             tiles_per_seq=tiles_per_seq, n_batch=n_batch)
    return pl.pallas_call(
        kern,
        grid=(t // tm, nb),
        in_specs=[pl.BlockSpec((tm, D), lambda m, n: (m, 0))] + w_specs + [
                  tile,
                  pl.BlockSpec((WRAP_B, tn), hist_index),
                  state,
                  pl.BlockSpec((CONV_B, SUBLANES, tn), lambda m, n: (0, 0, n)),
                  pl.BlockSpec((1, tn), lambda m, n: (0, n))],
        out_specs=[tile, tile, tile, state],
        out_shape=[jax.ShapeDtypeStruct((t, D), BF16), jax.ShapeDtypeStruct((t, D), F32),
                   jax.ShapeDtypeStruct((t, D), F32),
                   jax.ShapeDtypeStruct((CONV_B - 1, n_batch, D), F32)],
        scratch_shapes=[pltpu.VMEM((tm, D), BF16), pltpu.VMEM((HIST_B, tn), F32)],
        compiler_params=_params("arbitrary", "arbitrary"),
        name="in_proj_a",
    )(h, w_a, w_a, w_a, u, u, state_b_tm,
      jnp.broadcast_to(conv_b_w[:, None, :], (CONV_B, SUBLANES, D)), conv_b_bias)


def _conv_a_kernel(cv_ref, ba_ref, cw_ref, uw_ref, sa_ref, wa_ref, ya_o, pa_o, pb_o, nsa_o, xc_scr,
                   *, tc, n_prompt_tiles, tiles_per_seq, n_batch):
    i = pl.program_id(1)

    def taps(src_of_tap, rows):
        acc = None
        for k in range(CONV_A):
            term = src_of_tap(k) * wa_ref[k:k + 1, :]
            acc = term if acc is None else acc + term
        ya_o[rows, :] = (ba_ref[rows, :].astype(F32) * acc).astype(BF16)

    @pl.when(i < n_prompt_tiles)
    def _prompt():
        @pl.when(i % tiles_per_seq == 0)
        def _():
            w0 = cw_ref.shape[0] - HIST_A
            for j in range(CONV_A - 1):
                xc_scr[pl.ds(j * SUBLANES, SUBLANES), :] = _prev_segment(
                    cw_ref[pl.ds(w0 + j * SUBLANES, SUBLANES), :])

        @pl.when(i % tiles_per_seq != 0)
        def _():
            xc_scr[0:HIST_A, :] = xc_scr[tc:tc + HIST_A, :]

        xc_scr[HIST_A:HIST_A + tc, :] = cv_ref[...]

        def chunk(ci, carry):
            r0 = pl.multiple_of(ci * CONV_CHUNK, CONV_CHUNK)
            taps(lambda k: xc_scr[pl.ds(r0 + k * SUBLANES, CONV_CHUNK), :], pl.ds(r0, CONV_CHUNK))
            return carry

        lax.fori_loop(0, tc // CONV_CHUNK, chunk, 0)

        @pl.when(i % tiles_per_seq == tiles_per_seq - 1)
        def _():
            last = SUBLANES - 1
            for j in range(CONV_A - 1):
                pa_o[pl.ds(j, 1), :] = cv_ref[pl.ds(tc - HIST_A + j * SUBLANES + last, 1), :]
            for j in range(CONV_B - 1):
                pb_o[pl.ds(j, 1), :] = uw_ref[pl.ds(uw_ref.shape[0] - HIST_B + j * SUBLANES + last, 1), :]

    @pl.when(i == n_prompt_tiles)
    def _sample():
        n_t = tc // n_batch
        for t in range(n_t):
            def src_of_tap(k, t=t):
                j = t + k
                if j < CONV_A - 1:
                    return sa_ref[j]
                return cv_ref[pl.ds((j - (CONV_A - 1)) * n_batch, n_batch), :]

            taps(src_of_tap, pl.ds(t * n_batch, n_batch))
        for j in range(CONV_A - 1):
            jj = j + n_t
            if jj < CONV_A - 1:
                nsa_o[j] = sa_ref[jj]
            else:
                nsa_o[j] = cv_ref[pl.ds((jj - (CONV_A - 1)) * n_batch, n_batch), :]


def _conv_a(cv, ba, u, state_a_tm, conv_a_w, *, tc, cb, n_seq, seq, n_batch):
    t = cv.shape[0]
    n_prompt_tiles = (n_seq * seq) // tc
    tiles_per_seq = seq // tc
    assert t - n_seq * seq == tc and tc % n_batch == 0 and seq % tc == 0 and tc % CONV_CHUNK == 0

    def seq_of(i):
        return jnp.minimum(i // tiles_per_seq, n_seq - 1)

    tile = pl.BlockSpec((tc, cb), lambda c, i: (i, c))
    state = pl.BlockSpec((CONV_A - 1, n_batch, cb), lambda c, i: (0, 0, c))
    kern = functools.partial(_conv_a_kernel, tc=tc, n_prompt_tiles=n_prompt_tiles, tiles_per_seq=tiles_per_seq,
                             n_batch=n_batch)
    return pl.pallas_call(
        kern,
        grid=(D // cb, t // tc),
        in_specs=[tile, tile,
                  pl.BlockSpec((WRAP_A, cb), lambda c, i: ((seq_of(i) + 1) * (seq // WRAP_A) - 1, c)),
                  pl.BlockSpec((WRAP_B, cb), lambda c, i: ((seq_of(i) + 1) * (seq // WRAP_B) - 1, c)),
                  state,
                  pl.BlockSpec((CONV_A, cb), lambda c, i: (0, c))],
        out_specs=[tile,
                   pl.BlockSpec((None, CONV_A - 1, cb), lambda c, i: (seq_of(i), 0, c)),
                   pl.BlockSpec((None, CONV_B - 1, cb), lambda c, i: (seq_of(i), 0, c)),
                   state],
        out_shape=[jax.ShapeDtypeStruct((t, D), BF16),
                   jax.ShapeDtypeStruct((n_seq, CONV_A - 1, D), F32),
                   jax.ShapeDtypeStruct((n_seq, CONV_B - 1, D), F32),
                   jax.ShapeDtypeStruct((CONV_A - 1, n_batch, D), F32)],
        scratch_shapes=[pltpu.VMEM((HIST_A + tc, cb), F32)],
        compiler_params=_params("arbitrary", "arbitrary"),
        name="conv_a",
    )(cv, ba, cv, u, state_a_tm, conv_a_w)


def _mix_out_kernel(cb_ref, lg_ref, lb_ref, ya_ref, ga_ref, gb_ref, woa_ref, wob_ref, wo_ref, x_ref, o_ref, z_scr):
    c = cb_ref[...]
    mu = jnp.mean(c, axis=-1, keepdims=True)
    xc = c - mu
    var = jnp.mean(xc * xc, axis=-1, keepdims=True)
    y = xc * lax.rsqrt(var + LN_EPS) * lg_ref[...] + lb_ref[...]
    z_scr[...] = (y * jax.nn.sigmoid(y)).astype(BF16)

    y_a = jnp.dot(ya_ref[...], woa_ref[...], preferred_element_type=F32)
    y_b = jnp.dot(z_scr[...], wob_ref[...], preferred_element_type=F32)
    merged = (ga_ref[...].astype(F32) * y_a + gb_ref[...].astype(F32) * y_b).astype(BF16)
    o_ref[...] = x_ref[...] + jnp.dot(merged, wo_ref[...], preferred_element_type=F32)


def _mix_out(conv_b, ln_g, ln_b, ya_in, ga, gb, w_out_a, w_out_b, w_o, layer, x, *, tm):
    t = conv_b.shape[0]
    row = pl.BlockSpec((tm, D), lambda m: (m, 0))
    vec = pl.BlockSpec((1, D), lambda m: (0, 0))
    wfull = pl.BlockSpec((None, D, D), lambda m: (layer, 0, 0), pipeline_mode=pl.Buffered(1))
    return pl.pallas_call(
        _mix_out_kernel,
        grid=(t // tm,),
        in_specs=[row, vec, vec, row, row, row, wfull, wfull, wfull, row],
        out_specs=row,
        out_shape=jax.ShapeDtypeStruct((t, D), F32),
        scratch_shapes=[pltpu.VMEM((tm, D), BF16)],
        compiler_params=_params("arbitrary"),
        name="mix_out",
    )(conv_b, ln_g, ln_b, ya_in, ga, gb, w_out_a, w_out_b, w_o, x)


def _ffn_tile(x_ref, g_ref, wg_ref, wu_ref, wd_ref, gf_ref, o_ref, h_scr, final_norm):
    j = pl.program_id(1)

    @pl.when(j == 0)
    def _():
        x = x_ref[...]
        h_scr[...] = _rms(x, g_ref[...]).astype(BF16)
        o_ref[...] = x

    h = h_scr[...]
    gate = jnp.dot(h, wg_ref[...].astype(BF16), preferred_element_type=F32)
    up = jnp.dot(h, wu_ref[...].astype(BF16), preferred_element_type=F32)
    act = (gate * jax.nn.sigmoid(gate) * up).astype(BF16)
    o_ref[...] += jnp.dot(act, wd_ref[...].astype(BF16), preferred_element_type=F32)

    if final_norm:
        @pl.when(j == pl.num_programs(1) - 1)
        def _():
            o_ref[...] = _rms(o_ref[...], gf_ref[...])


def _ffn_kernel(x_ref, g_ref, wg_ref, wu_ref, wd_ref, gf_ref, o_ref, h_scr):
    _ffn_tile(x_ref, g_ref, wg_ref, wu_ref, wd_ref, gf_ref, o_ref, h_scr, False)


def _ffn_final_kernel(x_ref, g_ref, wg_ref, wu_ref, wd_ref, gf_ref, op_ref, os_ref, h_scr, *, n_prompt_tiles):
    m = pl.program_id(0)

    @pl.when(m < n_prompt_tiles)
    def _():
        _ffn_tile(x_ref, g_ref, wg_ref, wu_ref, wd_ref, gf_ref, op_ref, h_scr, True)

    @pl.when(m >= n_prompt_tiles)
    def _():
        _ffn_tile(x_ref, g_ref, wg_ref, wu_ref, wd_ref, gf_ref, os_ref, h_scr, True)


def _ffn(x, g, w_gate, w_up, w_down, layer, g_final, *, tm, tf, n_prompt=None):
    t = x.shape[0]
    vec = pl.BlockSpec((1, D), lambda m, j: (0, 0))
    if n_prompt is None:
        body = _ffn_kernel
        out_specs = pl.BlockSpec((tm, D), lambda m, j: (m, 0))
        out_shape = jax.ShapeDtypeStruct((t, D), F32)
    else:
        npt = n_prompt // tm
        body = functools.partial(_ffn_final_kernel, n_prompt_tiles=npt)
        out_specs = [pl.BlockSpec((tm, D), lambda m, j: (jnp.minimum(m, npt - 1), 0)),
                     pl.BlockSpec((tm, D), lambda m, j: (jnp.maximum(m - npt, 0), 0),
                                  pipeline_mode=pl.Buffered(1))]
        out_shape = [jax.ShapeDtypeStruct((n_prompt, D), F32), jax.ShapeDtypeStruct((t - n_prompt, D), F32)]
    return pl.pallas_call(
        body,
        grid=(t // tm, D_FF // tf),
        in_specs=[pl.BlockSpec((tm, D), lambda m, j: (m, 0)), vec,
                  pl.BlockSpec((None, D, tf), lambda m, j: (layer, 0, j)),
                  pl.BlockSpec((None, D, tf), lambda m, j: (layer, 0, j)),
                  pl.BlockSpec((None, tf, D), lambda m, j: (layer, j, 0)),
                  vec],
        out_specs=out_specs,
        out_shape=out_shape,
        scratch_shapes=[pltpu.VMEM((tm, D), BF16)],
        compiler_params=_params("arbitrary", "arbitrary"),
        name="ffn",
    )(x, g, w_gate, w_up, w_down, g_final)


def kernel(x_prompt, x_sample, state_conv_a, state_conv_b, norm_mix_g, w_in, conv_a_w, w_out_a, conv_b_w,
           conv_b_bias, ln_b_g, ln_b_b, w_out_b, w_o, norm_ffn_g, w_gate, w_up, w_down, final_norm_g):
    n_seq, seq, _ = x_prompt.shape
    n_batch, n_dec, _ = x_sample.shape
    depth = w_in.shape[0]
    n_prompt = n_seq * seq
    n_sample = n_batch * n_dec
    seg = seq // SUBLANES
    shape = dict(n_seq=n_seq, seq=seq, n_batch=n_batch)

    x = jnp.concatenate(
        [x_prompt.reshape(n_seq, SUBLANES, seg, D).transpose(0, 2, 1, 3).reshape(n_prompt, D),
         x_sample.transpose(1, 0, 2).reshape(n_sample, D)], axis=0)
    w_in, w_out_a, w_out_b, w_o = (w.astype(BF16) for w in (w_in, w_out_a, w_out_b, w_o))
    g_final = final_norm_g.reshape(1, D)
    pa, pb, sa, sb = [], [], [], []
    for l in range(depth):
        vec = lambda v: v[l].reshape(1, D)
        u, ga, gb, h = _in_proj_b(x, vec(norm_mix_g), w_in, l, tm=n_sample, tn=512)
        ba, cv, conv_b, new_sb = _in_proj_a(
            h, w_in, l, u, state_conv_b[l].transpose(1, 0, 2), conv_b_w[l], vec(conv_b_bias),
            tm=n_sample, tn=256, **shape)
        ya_in, new_pa, new_pb, new_sa = _conv_a(cv, ba, u, state_conv_a[l].transpose(1, 0, 2), conv_a_w[l],
                                                tc=n_sample, cb=1024, **shape)
        x = _mix_out(conv_b, vec(ln_b_g), vec(ln_b_b), ya_in, ga, gb, w_out_a, w_out_b, w_o, l, x, tm=256)
        x = _ffn(x, vec(norm_ffn_g), w_gate, w_up, w_down, l, g_final,
                 tm=1024, tf=256, n_prompt=(n_prompt if l == depth - 1 else None))
        pa.append(new_pa)
        pb.append(new_pb)
        sa.append(new_sa.transpose(1, 0, 2))
        sb.append(new_sb.transpose(1, 0, 2))

    y_p, y_s = x
    y_prompt = y_p.reshape(n_seq, seg, SUBLANES, D).transpose(0, 2, 1, 3).reshape(n_seq, seq, D)
    y_sample = y_s.reshape(n_dec, n_batch, D).transpose(1, 0, 2)
    return (y_prompt, y_sample, jnp.stack(pa), jnp.stack(pb), jnp.stack(sa), jnp.stack(sb))
```

```python
import functools

import jax
import jax.numpy as jnp
from jax import lax
from jax.experimental import pallas as pl
from jax.experimental.pallas import tpu as pltpu

D = 2048
D_FF = 5632
CONV_A = 3
CONV_B = 31
RMS_EPS = 1e-6
LN_EPS = 1e-5
SUBLANES = 8
MXU_SLICE_ROWS = 512
HIST_B = (CONV_B - 1) * SUBLANES
HIST_A = (CONV_A - 1) * SUBLANES
CONV_CHUNK = 32
WRAP_B = 256
WRAP_A = 16
VMEM_LIMIT = 58 * 1024 * 1024

BF16 = jnp.bfloat16
F32 = jnp.float32


def _params(*sem):
    return pltpu.CompilerParams(dimension_semantics=sem, vmem_limit_bytes=VMEM_LIMIT)


def _rms(x, g):
    return x * lax.rsqrt(jnp.mean(x * x, axis=-1, keepdims=True) + RMS_EPS) * g


def _prev_segment(rows):
    moved = pltpu.roll(rows, shift=1, axis=0)
    sub = lax.broadcasted_iota(jnp.int32, rows.shape, 0)
    return jnp.where(sub == 0, 0.0, moved)


def _in_proj_b_kernel(x_ref, g_ref, wga, wgb, wta, wtb, u_o, ga_o, gb_o, h_o):
    @pl.when(pl.program_id(1) == 0)
    def _():
        h_o[...] = _rms(x_ref[...], g_ref[...]).astype(BF16)

    h = h_o[...]

    def dot(w):
        return jnp.dot(h, w[...], preferred_element_type=F32)

    ga_o[...] = jax.nn.sigmoid(dot(wta)).astype(BF16)
    gb_o[...] = jax.nn.sigmoid(dot(wtb)).astype(BF16)
    glu_gate = jax.nn.sigmoid(dot(wgb))
    u_o[...] = dot(wga) * glu_gate


def _in_proj_b(x, g, w_in, layer, *, tm, tn):
    t = x.shape[0]
    nb = D // tn
    w_specs = [pl.BlockSpec((None, D, tn), functools.partial(lambda m, n, grp: (layer, 0, grp * nb + n), grp=grp))
               for grp in (3, 4, 5, 6)]
    tile = pl.BlockSpec((tm, tn), lambda m, n: (m, n))
    return pl.pallas_call(
        _in_proj_b_kernel,
        grid=(t // tm, nb),
        in_specs=[pl.BlockSpec((tm, D), lambda m, n: (m, 0)),
                  pl.BlockSpec((1, D), lambda m, n: (0, 0))] + w_specs,
        out_specs=[tile] * 3 + [pl.BlockSpec((tm, D), lambda m, n: (m, 0))],
        out_shape=[jax.ShapeDtypeStruct((t, D), F32), jax.ShapeDtypeStruct((t, D), BF16),
                   jax.ShapeDtypeStruct((t, D), BF16), jax.ShapeDtypeStruct((t, D), BF16)],
        compiler_params=_params("arbitrary", "arbitrary"),
        name="in_proj_b",
    )(x, g, w_in, w_in, w_in, w_in)


def _in_proj_a_kernel(h_ref, wb, wc, wv, u_ref, hist_ref, sb_ref, wb8_ref, bias_ref,
                      ba_o, cv_o, cb_o, nsb_o, h_scr, xh_scr,
                      *, tm, n_prompt_tiles, tiles_per_seq, n_batch):
    m = pl.program_id(0)
    groups = CONV_CHUNK // SUBLANES
    n_chunks = tm // CONV_CHUNK

    @pl.when(pl.program_id(1) == 0)
    def _():
        h_scr[...] = h_ref[...]

    def interleaved(conv_chunk):
        n_slices = tm // MXU_SLICE_ROWS
        weights = (wb, wc, wv)
        pieces = [(s, k) for s in range(n_slices) for k in range(len(weights))]
        bounds = [round(p * n_chunks / len(pieces)) for p in range(len(pieces) + 1)]
        for p, (s, k) in enumerate(pieces):
            for ci in range(bounds[p], bounds[p + 1]):
                conv_chunk(ci)
            rows = pl.ds(s * MXU_SLICE_ROWS, MXU_SLICE_ROWS)
            y = jnp.dot(h_scr[rows, :], weights[k][...], preferred_element_type=F32)
            if k == 0:
                ba_o[rows, :] = y.astype(BF16)
            elif k == 1:
                cv_o[rows, :] = y
            else:
                cv_o[rows, :] = cv_o[rows, :] * y

    @pl.when(m < n_prompt_tiles)
    def _prompt():
        first = m % tiles_per_seq == 0
        w0 = hist_ref.shape[0] - HIST_B
        for j in range(CONV_B - 1):
            rows = hist_ref[pl.ds(w0 + j * SUBLANES, SUBLANES), :]
            xh_scr[pl.ds(j * SUBLANES, SUBLANES), :] = jnp.where(first, _prev_segment(rows), rows)
        bias8 = jnp.broadcast_to(bias_ref[...], (SUBLANES, bias_ref.shape[1]))

        def conv_chunk(ci):
            c0 = ci * CONV_CHUNK
            accs = [bias8] * groups
            for k in range(CONV_B):
                w = wb8_ref[k]
                for gi in range(groups):
                    r = c0 + (gi + k) * SUBLANES - HIST_B
                    if r < 0:
                        src = xh_scr[pl.ds(r + HIST_B, SUBLANES), :]
                    else:
                        src = u_ref[pl.ds(r, SUBLANES), :]
                    accs[gi] = accs[gi] + src * w
            for gi in range(groups):
                cb_o[pl.ds(c0 + gi * SUBLANES, SUBLANES), :] = accs[gi]
            return accs[-1]

        interleaved(conv_chunk)

    @pl.when(m == n_prompt_tiles)
    def _sample():
        n_t = tm // n_batch
        bias8 = jnp.broadcast_to(bias_ref[...], (SUBLANES, bias_ref.shape[1]))

        def conv_chunk(ci):
            t, r0 = divmod(ci * CONV_CHUNK, n_batch)
            accs = [bias8] * groups
            for k in range(CONV_B):
                j = t + k
                w = wb8_ref[k]
                for gi in range(groups):
                    g0 = r0 + gi * SUBLANES
                    if j < CONV_B - 1:
                        src = sb_ref[j, pl.ds(g0, SUBLANES), :]
                    else:
                        src = u_ref[pl.ds((j - (CONV_B - 1)) * n_batch + g0, SUBLANES), :]
                    accs[gi] = accs[gi] + src * w
            for gi in range(groups):
                cb_o[pl.ds(t * n_batch + r0 + gi * SUBLANES, SUBLANES), :] = accs[gi]
            return accs[-1]

        interleaved(conv_chunk)
        for j in range(CONV_B - 1):
            jj = j + n_t
            if jj < CONV_B - 1:
                nsb_o[j] = sb_ref[jj]
            else:
                nsb_o[j] = u_ref[pl.ds((jj - (CONV_B - 1)) * n_batch, n_batch), :]


def _in_proj_a(h, w_a, layer, u, state_b_tm, conv_b_w, conv_b_bias, *, tm, tn, n_seq, seq, n_batch):
    t = h.shape[0]
    nb = D // tn
    n_prompt_tiles = (n_seq * seq) // tm
    tiles_per_seq = seq // tm
    assert t - n_seq * seq == tm and tm % n_batch == 0 and seq % tm == 0 and tm % WRAP_B == 0
    per_tile = tm // WRAP_B

    def hist_index(m, n):
        seq_last = (jnp.minimum(m // tiles_per_seq, n_seq - 1) + 1) * (seq // WRAP_B) - 1
        return jnp.where(m % tiles_per_seq == 0, seq_last, m * per_tile - 1), n

    def sample_only(m, n):
        return jnp.where(m == n_prompt_tiles, n, 0)

    w_specs = [pl.BlockSpec((None, D, tn), functools.partial(lambda m, n, grp: (layer, 0, grp * nb + n), grp=grp))
               for grp in range(3)]
    tile = pl.BlockSpec((tm, tn), lambda m, n: (m, n))
    state = pl.BlockSpec((CONV_B - 1, n_batch, tn), lambda m, n: (0, 0, sample_only(m, n)))
    kern = functools.partial(_in_proj_a_kernel, tm=tm, n_prompt_tiles=n_prompt_tiles,
                             tiles_per_seq=tiles_per_seq, n_batch=n_batch)
    return pl.pallas_call(
        kern,
        grid=(t // tm, nb),
        in_specs=[pl.BlockSpec((tm, D), lambda m, n: (m, 0))] + w_specs + [
                  tile,
                  pl.BlockSpec((WRAP_B, tn), hist_index),
                  state,
                  pl.BlockSpec((CONV_B, SUBLANES, tn), lambda m, n: (0, 0, n)),
                  pl.BlockSpec((1, tn), lambda m, n: (0, n))],
        out_specs=[tile, tile, tile, state],
        out_shape=[jax.ShapeDtypeStruct((t, D), BF16), jax.ShapeDtypeStruct((t, D), F32),
                   jax.ShapeDtypeStruct((t, D), F32),
                   jax.ShapeDtypeStruct((CONV_B - 1, n_batch, D), F32)],
        scratch_shapes=[pltpu.VMEM((tm, D), BF16), pltpu.VMEM((HIST_B, tn), F32)],
        compiler_params=_params("arbitrary", "arbitrary"),
        name="in_proj_a",
    )(h, w_a, w_a, w_a, u, u, state_b_tm,
      jnp.broadcast_to(conv_b_w[:, None, :], (CONV_B, SUBLANES, D)), conv_b_bias)


def _conv_a_kernel(cv_ref, ba_ref, cw_ref, uw_ref, sa_ref, wa_ref, ya_o, pa_o, pb_o, nsa_o, xc_scr,
                   *, tc, n_prompt_tiles, tiles_per_seq, n_batch):
    i = pl.program_id(1)

    def taps(src_of_tap, rows):
        acc = None
        for k in range(CONV_A):
            term = src_of_tap(k) * wa_ref[k:k + 1, :]
            acc = term if acc is None else acc + term
        ya_o[rows, :] = (ba_ref[rows, :].astype(F32) * acc).astype(BF16)

    @pl.when(i < n_prompt_tiles)
    def _prompt():
        @pl.when(i % tiles_per_seq == 0)
        def _():
            w0 = cw_ref.shape[0] - HIST_A
            for j in range(CONV_A - 1):
                xc_scr[pl.ds(j * SUBLANES, SUBLANES), :] = _prev_segment(
                    cw_ref[pl.ds(w0 + j * SUBLANES, SUBLANES), :])

        @pl.when(i % tiles_per_seq != 0)
        def _():
            xc_scr[0:HIST_A, :] = xc_scr[tc:tc + HIST_A, :]

        xc_scr[HIST_A:HIST_A + tc, :] = cv_ref[...]

        def chunk(ci, carry):
            r0 = pl.multiple_of(ci * CONV_CHUNK, CONV_CHUNK)
            taps(lambda k: xc_scr[pl.ds(r0 + k * SUBLANES, CONV_CHUNK), :], pl.ds(r0, CONV_CHUNK))
            return carry

        lax.fori_loop(0, tc // CONV_CHUNK, chunk, 0)

        @pl.when(i % tiles_per_seq == tiles_per_seq - 1)
        def _():
            last = SUBLANES - 1
            for j in range(CONV_A - 1):
                pa_o[pl.ds(j, 1), :] = cv_ref[pl.ds(tc - HIST_A + j * SUBLANES + last, 1), :]
            for j in range(CONV_B - 1):
                pb_o[pl.ds(j, 1), :] = uw_ref[pl.ds(uw_ref.shape[0] - HIST_B + j * SUBLANES + last, 1), :]

    @pl.when(i == n_prompt_tiles)
    def _sample():
        n_t = tc // n_batch
        for t in range(n_t):
            def src_of_tap(k, t=t):
                j = t + k
                if j < CONV_A - 1:
                    return sa_ref[j]
                return cv_ref[pl.ds((j - (CONV_A - 1)) * n_batch, n_batch), :]

            taps(src_of_tap, pl.ds(t * n_batch, n_batch))
        for j in range(CONV_A - 1):
            jj = j + n_t
            if jj < CONV_A - 1:
                nsa_o[j] = sa_ref[jj]
            else:
                nsa_o[j] = cv_ref[pl.ds((jj - (CONV_A - 1)) * n_batch, n_batch), :]


def _conv_a(cv, ba, u, state_a_tm, conv_a_w, *, tc, cb, n_seq, seq, n_batch):
    t = cv.shape[0]
    n_prompt_tiles = (n_seq * seq) // tc
    tiles_per_seq = seq // tc
    assert t - n_seq * seq == tc and tc % n_batch == 0 and seq % tc == 0 and tc % CONV_CHUNK == 0

    def seq_of(i):
        return jnp.minimum(i // tiles_per_seq, n_seq - 1)

    tile = pl.BlockSpec((tc, cb), lambda c, i: (i, c))
    state = pl.BlockSpec((CONV_A - 1, n_batch, cb), lambda c, i: (0, 0, c))
    kern = functools.partial(_conv_a_kernel, tc=tc, n_prompt_tiles=n_prompt_tiles, tiles_per_seq=tiles_per_seq,
                             n_batch=n_batch)
    return pl.pallas_call(
        kern,
        grid=(D // cb, t // tc),
        in_specs=[tile, tile,
                  pl.BlockSpec((WRAP_A, cb), lambda c, i: ((seq_of(i) + 1) * (seq // WRAP_A) - 1, c)),
                  pl.BlockSpec((WRAP_B, cb), lambda c, i: ((seq_of(i) + 1) * (seq // WRAP_B) - 1, c)),
                  state,
                  pl.BlockSpec((CONV_A, cb), lambda c, i: (0, c))],
        out_specs=[tile,
                   pl.BlockSpec((None, CONV_A - 1, cb), lambda c, i: (seq_of(i), 0, c)),
                   pl.BlockSpec((None, CONV_B - 1, cb), lambda c, i: (seq_of(i), 0, c)),
                   state],
        out_shape=[jax.ShapeDtypeStruct((t, D), BF16),
                   jax.ShapeDtypeStruct((n_seq, CONV_A - 1, D), F32),
                   jax.ShapeDtypeStruct((n_seq, CONV_B - 1, D), F32),
                   jax.ShapeDtypeStruct((CONV_A - 1, n_batch, D), F32)],
        scratch_shapes=[pltpu.VMEM((HIST_A + tc, cb), F32)],
        compiler_params=_params("arbitrary", "arbitrary"),
        name="conv_a",
    )(cv, ba, cv, u, state_a_tm, conv_a_w)


def _mix_out_kernel(cb_ref, lg_ref, lb_ref, ya_ref, ga_ref, gb_ref, woa_ref, wob_ref, wo_ref, x_ref, o_ref, z_scr):
    c = cb_ref[...]
    mu = jnp.mean(c, axis=-1, keepdims=True)
    xc = c - mu
    var = jnp.mean(xc * xc, axis=-1, keepdims=True)
    y = xc * lax.rsqrt(var + LN_EPS) * lg_ref[...] + lb_ref[...]
    z_scr[...] = (y * jax.nn.sigmoid(y)).astype(BF16)

    y_a = jnp.dot(ya_ref[...], woa_ref[...], preferred_element_type=F32)
    y_b = jnp.dot(z_scr[...], wob_ref[...], preferred_element_type=F32)
    merged = (ga_ref[...].astype(F32) * y_a + gb_ref[...].astype(F32) * y_b).astype(BF16)
    o_ref[...] = x_ref[...] + jnp.dot(merged, wo_ref[...], preferred_element_type=F32)


def _mix_out(conv_b, ln_g, ln_b, ya_in, ga, gb, w_out_a, w_out_b, w_o, layer, x, *, tm):
    t = conv_b.shape[0]
    row = pl.BlockSpec((tm, D), lambda m: (m, 0))
    vec = pl.BlockSpec((1, D), lambda m: (0, 0))
    wfull = pl.BlockSpec((None, D, D), lambda m: (layer, 0, 0), pipeline_mode=pl.Buffered(1))
    return pl.pallas_call(
        _mix_out_kernel,
        grid=(t // tm,),
        in_specs=[row, vec, vec, row, row, row, wfull, wfull, wfull, row],
        out_specs=row,
        out_shape=jax.ShapeDtypeStruct((t, D), F32),
        scratch_shapes=[pltpu.VMEM((tm, D), BF16)],
        compiler_params=_params("arbitrary"),
        name="mix_out",
    )(conv_b, ln_g, ln_b, ya_in, ga, gb, w_out_a, w_out_b, w_o, x)


def _ffn_tile(x_ref, g_ref, wg_ref, wu_ref, wd_ref, gf_ref, o_ref, h_scr, final_norm):
    j = pl.program_id(1)

    @pl.when(j == 0)
    def _():
        x = x_ref[...]
        h_scr[...] = _rms(x, g_ref[...]).astype(BF16)
        o_ref[...] = x

    h = h_scr[...]
    gate = jnp.dot(h, wg_ref[...].astype(BF16), preferred_element_type=F32)
    up = jnp.dot(h, wu_ref[...].astype(BF16), preferred_element_type=F32)
    act = (gate * jax.nn.sigmoid(gate) * up).astype(BF16)
    o_ref[...] += jnp.dot(act, wd_ref[...].astype(BF16), preferred_element_type=F32)

    if final_norm:
        @pl.when(j == pl.num_programs(1) - 1)
        def _():
            o_ref[...] = _rms(o_ref[...], gf_ref[...])


def _ffn_kernel(x_ref, g_ref, wg_ref, wu_ref, wd_ref, gf_ref, o_ref, h_scr):
    _ffn_tile(x_ref, g_ref, wg_ref, wu_ref, wd_ref, gf_ref, o_ref, h_scr, False)


def _ffn_final_kernel(x_ref, g_ref, wg_ref, wu_ref, wd_ref, gf_ref, op_ref, os_ref, h_scr, *, n_prompt_tiles):
    m = pl.program_id(0)

    @pl.when(m < n_prompt_tiles)
    def _():
        _ffn_tile(x_ref, g_ref, wg_ref, wu_ref, wd_ref, gf_ref, op_ref, h_scr, True)

    @pl.when(m >= n_prompt_tiles)
    def _():
        _ffn_tile(x_ref, g_ref, wg_ref, wu_ref, wd_ref, gf_ref, os_ref, h_scr, True)


def _ffn(x, g, w_gate, w_up, w_down, layer, g_final, *, tm, tf, n_prompt=None):
    t = x.shape[0]
    vec = pl.BlockSpec((1, D), lambda m, j: (0, 0))
    if n_prompt is None:
        body = _ffn_kernel
        out_specs = pl.BlockSpec((tm, D), lambda m, j: (m, 0))
        out_shape = jax.ShapeDtypeStruct((t, D), F32)
    else:
        npt = n_prompt // tm
        body = functools.partial(_ffn_final_kernel, n_prompt_tiles=npt)
        out_specs = [pl.BlockSpec((tm, D), lambda m, j: (jnp.minimum(m, npt - 1), 0)),
                     pl.BlockSpec((tm, D), lambda m, j: (jnp.maximum(m - npt, 0), 0),
                                  pipeline_mode=pl.Buffered(1))]
        out_shape = [jax.ShapeDtypeStruct((n_prompt, D), F32), jax.ShapeDtypeStruct((t - n_prompt, D), F32)]
    return pl.pallas_call(
        body,
        grid=(t // tm, D_FF // tf),
        in_specs=[pl.BlockSpec((tm, D), lambda m, j: (m, 0)), vec,
                  pl.BlockSpec((None, D, tf), lambda m, j: (layer, 0, j)),
                  pl.BlockSpec((None, D, tf), lambda m, j: (layer, 0, j)),
                  pl.BlockSpec((None, tf, D), lambda m, j: (layer, j, 0)),
                  vec],
        out_specs=out_specs,
        out_shape=out_shape,
        scratch_shapes=[pltpu.VMEM((tm, D), BF16)],
        compiler_params=_params("arbitrary", "arbitrary"),
        name="ffn",
    )(x, g, w_gate, w_up, w_down, g_final)


def kernel(x_prompt, x_sample, state_conv_a, state_conv_b, norm_mix_g, w_in, conv_a_w, w_out_a, conv_b_w,
           conv_b_bias, ln_b_g, ln_b_b, w_out_b, w_o, norm_ffn_g, w_gate, w_up, w_down, final_norm_g):
    n_seq, seq, _ = x_prompt.shape
    n_batch, n_dec, _ = x_sample.shape
    depth = w_in.shape[0]
    n_prompt = n_seq * seq
    n_sample = n_batch * n_dec
    seg = seq // SUBLANES
    shape = dict(n_seq=n_seq, seq=seq, n_batch=n_batch)

    x = jnp.concatenate(
        [x_prompt.reshape(n_seq, SUBLANES, seg, D).transpose(0, 2, 1, 3).reshape(n_prompt, D),
         x_sample.transpose(1, 0, 2).reshape(n_sample, D)], axis=0)
    w_in, w_out_a, w_out_b, w_o = (w.astype(BF16) for w in (w_in, w_out_a, w_out_b, w_o))
    g_final = final_norm_g.reshape(1, D)
    pa, pb, sa, sb = [], [], [], []
    for l in range(depth):
        vec = lambda v: v[l].reshape(1, D)
        u, ga, gb, h = _in_proj_b(x, vec(norm_mix_g), w_in, l, tm=n_sample, tn=512)
        ba, cv, conv_b, new_sb = _in_proj_a(
            h, w_in, l, u, state_conv_b[l].transpose(1, 0, 2), conv_b_w[l], vec(conv_b_bias),
            tm=n_sample, tn=256, **shape)
        ya_in, new_pa, new_pb, new_sa = _conv_a(cv, ba, u, state_conv_a[l].transpose(1, 0, 2), conv_a_w[l],
                                                tc=n_sample, cb=1024, **shape)
        x = _mix_out(conv_b, vec(ln_b_g), vec(ln_b_b), ya_in, ga, gb, w_out_a, w_out_b, w_o, l, x, tm=384)
        x = _ffn(x, vec(norm_ffn_g), w_gate, w_up, w_down, l, g_final,
                 tm=1024, tf=256, n_prompt=(n_prompt if l == depth - 1 else None))
        pa.append(new_pa)
        pb.append(new_pb)
        sa.append(new_sa.transpose(1, 0, 2))
        sb.append(new_sb.transpose(1, 0, 2))

    y_p, y_s = x
    y_prompt = y_p.reshape(n_seq, seg, SUBLANES, D).transpose(0, 2, 1, 3).reshape(n_seq, seq, D)
    y_sample = y_s.reshape(n_dec, n_batch, D).transpose(1, 0, 2)
    return (y_prompt, y_sample, jnp.stack(pa), jnp.stack(pb), jnp.stack(sa), jnp.stack(sb))
```

```python
import functools

import jax
import jax.numpy as jnp
from jax import lax
from jax.experimental import pallas as pl
from jax.experimental.pallas import tpu as pltpu

D = 2048
D_FF = 5632
CONV_A = 3
CONV_B = 31
RMS_EPS = 1e-6
LN_EPS = 1e-5
SUBLANES = 8
MXU_SLICE_ROWS = 512
HIST_B = (CONV_B - 1) * SUBLANES
HIST_A = (CONV_A - 1) * SUBLANES
CONV_CHUNK = 32
WRAP_B = 256
WRAP_A = 16
VMEM_LIMIT = 58 * 1024 * 1024

BF16 = jnp.bfloat16
F32 = jnp.float32


def _params(*sem):
    return pltpu.CompilerParams(dimension_semantics=sem, vmem_limit_bytes=VMEM_LIMIT)


def _rms(x, g):
    return x * lax.rsqrt(jnp.mean(x * x, axis=-1, keepdims=True) + RMS_EPS) * g


def _prev_segment(rows):
    moved = pltpu.roll(rows, shift=1, axis=0)
    sub = lax.broadcasted_iota(jnp.int32, rows.shape, 0)
    return jnp.where(sub == 0, 0.0, moved)


def _in_proj_b_kernel(x_ref, g_ref, wga, wgb, wta, wtb, u_o, ga_o, gb_o, h_o):
    @pl.when(pl.program_id(1) == 0)
    def _():
        h_o[...] = _rms(x_ref[...], g_ref[...]).astype(BF16)

    h = h_o[...]

    def dot(w):
        return jnp.dot(h, w[...], preferred_element_type=F32)

    ga_o[...] = jax.nn.sigmoid(dot(wta)).astype(BF16)
    gb_o[...] = jax.nn.sigmoid(dot(wtb)).astype(BF16)
    glu_gate = jax.nn.sigmoid(dot(wgb))
    u_o[...] = dot(wga) * glu_gate


def _in_proj_b(x, g, w_in, layer, *, tm, tn):
    t = x.shape[0]
    nb = D // tn
    w_specs = [pl.BlockSpec((None, D, tn), functools.partial(lambda m, n, grp: (layer, 0, grp * nb + n), grp=grp))
               for grp in (3, 4, 5, 6)]
    tile = pl.BlockSpec((tm, tn), lambda m, n: (m, n))
    return pl.pallas_call(
        _in_proj_b_kernel,
        grid=(t // tm, nb),
        in_specs=[pl.BlockSpec((tm, D), lambda m, n: (m, 0)),
                  pl.BlockSpec((1, D), lambda m, n: (0, 0))] + w_specs,
        out_specs=[tile] * 3 + [pl.BlockSpec((tm, D), lambda m, n: (m, 0))],
        out_shape=[jax.ShapeDtypeStruct((t, D), F32), jax.ShapeDtypeStruct((t, D), BF16),
                   jax.ShapeDtypeStruct((t, D), BF16), jax.ShapeDtypeStruct((t, D), BF16)],
        compiler_params=_params("arbitrary", "arbitrary"),
        name="in_proj_b",
    )(x, g, w_in, w_in, w_in, w_in)


def _in_proj_a_kernel(h_ref, wb, wc, wv, u_ref, hist_ref, sb_ref, wb8_ref, bias_ref,
                      ba_o, cv_o, cb_o, nsb_o, h_scr, xh_scr,
                      *, tm, n_prompt_tiles, tiles_per_seq, n_batch):
    m = pl.program_id(0)
    groups = CONV_CHUNK // SUBLANES
    n_chunks = tm // CONV_CHUNK

    @pl.when(pl.program_id(1) == 0)
    def _():
        h_scr[...] = h_ref[...]

    def interleaved(conv_chunk):
        n_slices = tm // MXU_SLICE_ROWS
        weights = (wb, wc, wv)
        pieces = [(s, k) for s in range(n_slices) for k in range(len(weights))]
        bounds = [round(p * n_chunks / len(pieces)) for p in range(len(pieces) + 1)]
        for p, (s, k) in enumerate(pieces):
            for ci in range(bounds[p], bounds[p + 1]):
                conv_chunk(ci)
            rows = pl.ds(s * MXU_SLICE_ROWS, MXU_SLICE_ROWS)
            y = jnp.dot(h_scr[rows, :], weights[k][...], preferred_element_type=F32)
            if k == 0:
                ba_o[rows, :] = y.astype(BF16)
            elif k == 1:
                cv_o[rows, :] = y
            else:
                cv_o[rows, :] = cv_o[rows, :] * y

    @pl.when(m < n_prompt_tiles)
    def _prompt():
        first = m % tiles_per_seq == 0
        w0 = hist_ref.shape[0] - HIST_B
        for j in range(CONV_B - 1):
            rows = hist_ref[pl.ds(w0 + j * SUBLANES, SUBLANES), :]
            xh_scr[pl.ds(j * SUBLANES, SUBLANES), :] = jnp.where(first, _prev_segment(rows), rows)
        bias8 = jnp.broadcast_to(bias_ref[...], (SUBLANES, bias_ref.shape[1]))

        def conv_chunk(ci):
            c0 = ci * CONV_CHUNK
            accs = [bias8] * groups
            for k in range(CONV_B):
                w = wb8_ref[k]
                for gi in range(groups):
                    r = c0 + (gi + k) * SUBLANES - HIST_B
                    if r < 0:
                        src = xh_scr[pl.ds(r + HIST_B, SUBLANES), :]
                    else:
                        src = u_ref[pl.ds(r, SUBLANES), :]
                    accs[gi] = accs[gi] + src * w
            for gi in range(groups):
                cb_o[pl.ds(c0 + gi * SUBLANES, SUBLANES), :] = accs[gi]
            return accs[-1]

        interleaved(conv_chunk)

    @pl.when(m == n_prompt_tiles)
    def _sample():
        n_t = tm // n_batch
        bias8 = jnp.broadcast_to(bias_ref[...], (SUBLANES, bias_ref.shape[1]))

        def conv_chunk(ci):
            t, r0 = divmod(ci * CONV_CHUNK, n_batch)
            accs = [bias8] * groups
            for k in range(CONV_B):
                j = t + k
                w = wb8_ref[k]
                for gi in range(groups):
                    g0 = r0 + gi * SUBLANES
                    if j < CONV_B - 1:
                        src = sb_ref[j, pl.ds(g0, SUBLANES), :]
                    else:
                        src = u_ref[pl.ds((j - (CONV_B - 1)) * n_batch + g0, SUBLANES), :]
                    accs[gi] = accs[gi] + src * w
            for gi in range(groups):
                cb_o[pl.ds(t * n_batch + r0 + gi * SUBLANES, SUBLANES), :] = accs[gi]
            return accs[-1]

        interleaved(conv_chunk)
        for j in range(CONV_B - 1):
            jj = j + n_t
            if jj < CONV_B - 1:
                nsb_o[j] = sb_ref[jj]
            else:
                nsb_o[j] = u_ref[pl.ds((jj - (CONV_B - 1)) * n_batch, n_batch), :]


def _in_proj_a(h, w_a, layer, u, state_b_tm, conv_b_w, conv_b_bias, *, tm, tn, n_seq, seq, n_batch):
    t = h.shape[0]
    nb = D // tn
    n_prompt_tiles = (n_seq * seq) // tm
    tiles_per_seq = seq // tm
    assert t - n_seq * seq == tm and tm % n_batch == 0 and seq % tm == 0 and tm % WRAP_B == 0
    per_tile = tm // WRAP_B

    def hist_index(m, n):
        seq_last = (jnp.minimum(m // tiles_per_seq, n_seq - 1) + 1) * (seq // WRAP_B) - 1
        return jnp.where(m % tiles_per_seq == 0, seq_last, m * per_tile - 1), n

    def sample_only(m, n):
        return jnp.where(m == n_prompt_tiles, n, 0)

    w_specs = [pl.BlockSpec((None, D, tn), functools.partial(lambda m, n, grp: (layer, 0, grp * nb + n), grp=grp))
               for grp in range(3)]
    tile = pl.BlockSpec((tm, tn), lambda m, n: (m, n))
    state = pl.BlockSpec((CONV_B - 1, n_batch, tn), lambda m, n: (0, 0, sample_only(m, n)))
    kern = functools.partial(_in_proj_a_kernel, tm=tm, n_prompt_tiles=n_prompt_tiles,
                             tiles_per_seq=tiles_per_seq, n_batch=n_batch)
    return pl.pallas_call(
        kern,
        grid=(t // tm, nb),
        in_specs=[pl.BlockSpec((tm, D), lambda m, n: (m, 0))] + w_specs + [
                  tile,
                  pl.BlockSpec((WRAP_B, tn), hist_index),
                  state,
                  pl.BlockSpec((CONV_B, SUBLANES, tn), lambda m, n: (0, 0, n)),
                  pl.BlockSpec((1, tn), lambda m, n: (0, n))],
        out_specs=[tile, tile, tile, state],
        out_shape=[jax.ShapeDtypeStruct((t, D), BF16), jax.ShapeDtypeStruct((t, D), F32),
                   jax.ShapeDtypeStruct((t, D), F32),
                   jax.ShapeDtypeStruct((CONV_B - 1, n_batch, D), F32)],
        scratch_shapes=[pltpu.VMEM((tm, D), BF16), pltpu.VMEM((HIST_B, tn), F32)],
        compiler_params=_params("arbitrary", "arbitrary"),
        name="in_proj_a",
    )(h, w_a, w_a, w_a, u, u, state_b_tm,
      jnp.broadcast_to(conv_b_w[:, None, :], (CONV_B, SUBLANES, D)), conv_b_bias)


def _conv_a_kernel(cv_ref, ba_ref, cw_ref, uw_ref, sa_ref, wa_ref, ya_o, pa_o, pb_o, nsa_o, xc_scr,
                   *, tc, n_prompt_tiles, tiles_per_seq, n_batch):
    i = pl.program_id(1)

    def taps(src_of_tap, rows):
        acc = None
        for k in range(CONV_A):
            term = src_of_tap(k) * wa_ref[k:k + 1, :]
            acc = term if acc is None else acc + term
        ya_o[rows, :] = (ba_ref[rows, :].astype(F32) * acc).astype(BF16)

    @pl.when(i < n_prompt_tiles)
    def _prompt():
        @pl.when(i % tiles_per_seq == 0)
        def _():
            w0 = cw_ref.shape[0] - HIST_A
            for j in range(CONV_A - 1):
                xc_scr[pl.ds(j * SUBLANES, SUBLANES), :] = _prev_segment(
                    cw_ref[pl.ds(w0 + j * SUBLANES, SUBLANES), :])

        @pl.when(i % tiles_per_seq != 0)
        def _():
            xc_scr[0:HIST_A, :] = xc_scr[tc:tc + HIST_A, :]

        xc_scr[HIST_A:HIST_A + tc, :] = cv_ref[...]

        def chunk(ci, carry):
            r0 = pl.multiple_of(ci * CONV_CHUNK, CONV_CHUNK)
            taps(lambda k: xc_scr[pl.ds(r0 + k * SUBLANES, CONV_CHUNK), :], pl.ds(r0, CONV_CHUNK))
            return carry

        lax.fori_loop(0, tc // CONV_CHUNK, chunk, 0)

        @pl.when(i % tiles_per_seq == tiles_per_seq - 1)
        def _():
            last = SUBLANES - 1
            for j in range(CONV_A - 1):
                pa_o[pl.ds(j, 1), :] = cv_ref[pl.ds(tc - HIST_A + j * SUBLANES + last, 1), :]
            for j in range(CONV_B - 1):
                pb_o[pl.ds(j, 1), :] = uw_ref[pl.ds(uw_ref.shape[0] - HIST_B + j * SUBLANES + last, 1), :]

    @pl.when(i == n_prompt_tiles)
    def _sample():
        n_t = tc // n_batch
        for t in range(n_t):
            def src_of_tap(k, t=t):
                j = t + k
                if j < CONV_A - 1:
                    return sa_ref[j]
                return cv_ref[pl.ds((j - (CONV_A - 1)) * n_batch, n_batch), :]

            taps(src_of_tap, pl.ds(t * n_batch, n_batch))
        for j in range(CONV_A - 1):
            jj = j + n_t
            if jj < CONV_A - 1:
                nsa_o[j] = sa_ref[jj]
            else:
                nsa_o[j] = cv_ref[pl.ds((jj - (CONV_A - 1)) * n_batch, n_batch), :]


def _conv_a(cv, ba, u, state_a_tm, conv_a_w, *, tc, cb, n_seq, seq, n_batch):
    t = cv.shape[0]
    n_prompt_tiles = (n_seq * seq) // tc
    tiles_per_seq = seq // tc
    assert t - n_seq * seq == tc and tc % n_batch == 0 and seq % tc == 0 and tc % CONV_CHUNK == 0

    def seq_of(i):
        return jnp.minimum(i // tiles_per_seq, n_seq - 1)

    tile = pl.BlockSpec((tc, cb), lambda c, i: (i, c))
    state = pl.BlockSpec((CONV_A - 1, n_batch, cb), lambda c, i: (0, 0, c))
    kern = functools.partial(_conv_a_kernel, tc=tc, n_prompt_tiles=n_prompt_tiles, tiles_per_seq=tiles_per_seq,
                             n_batch=n_batch)
    return pl.pallas_call(
        kern,
        grid=(D // cb, t // tc),
        in_specs=[tile, tile,
                  pl.BlockSpec((WRAP_A, cb), lambda c, i: ((seq_of(i) + 1) * (seq // WRAP_A) - 1, c)),
                  pl.BlockSpec((WRAP_B, cb), lambda c, i: ((seq_of(i) + 1) * (seq // WRAP_B) - 1, c)),
                  state,
                  pl.BlockSpec((CONV_A, cb), lambda c, i: (0, c))],
        out_specs=[tile,
                   pl.BlockSpec((None, CONV_A - 1, cb), lambda c, i: (seq_of(i), 0, c)),
                   pl.BlockSpec((None, CONV_B - 1, cb), lambda c, i: (seq_of(i), 0, c)),
                   state],
        out_shape=[jax.ShapeDtypeStruct((t, D), BF16),
                   jax.ShapeDtypeStruct((n_seq, CONV_A - 1, D), F32),
                   jax.ShapeDtypeStruct((n_seq, CONV_B - 1, D), F32),
                   jax.ShapeDtypeStruct((CONV_A - 1, n_batch, D), F32)],
        scratch_shapes=[pltpu.VMEM((HIST_A + tc, cb), F32)],
        compiler_params=_params("arbitrary", "arbitrary"),
        name="conv_a",
    )(cv, ba, cv, u, state_a_tm, conv_a_w)


def _mix_out_kernel(cb_ref, lg_ref, lb_ref, ya_ref, ga_ref, gb_ref, woa_ref, wob_ref, wo_ref, x_ref, o_ref, z_scr):
    c = cb_ref[...]
    mu = jnp.mean(c, axis=-1, keepdims=True)
    xc = c - mu
    var = jnp.mean(xc * xc, axis=-1, keepdims=True)
    y = xc * lax.rsqrt(var + LN_EPS) * lg_ref[...] + lb_ref[...]
    z_scr[...] = (y * jax.nn.sigmoid(y)).astype(BF16)

    y_a = jnp.dot(ya_ref[...], woa_ref[...], preferred_element_type=F32)
    y_b = jnp.dot(z_scr[...], wob_ref[...], preferred_element_type=F32)
    merged = (ga_ref[...].astype(F32) * y_a + gb_ref[...].astype(F32) * y_b).astype(BF16)
    o_ref[...] = x_ref[...] + jnp.dot(merged, wo_ref[...], preferred_element_type=F32)


def _merge_kernel(cb_ref, lg_ref, lb_ref, ya_ref, ga_ref, gb_ref, woa_ref, wob_ref, m_o, z_scr):
    c = cb_ref[...]
    mu = jnp.mean(c, axis=-1, keepdims=True)
    xc = c - mu
    var = jnp.mean(xc * xc, axis=-1, keepdims=True)
    y = xc * lax.rsqrt(var + LN_EPS) * lg_ref[...] + lb_ref[...]
    z_scr[...] = (y * jax.nn.sigmoid(y)).astype(BF16)
    y_a = jnp.dot(ya_ref[...], woa_ref[...], preferred_element_type=F32)
    y_b = jnp.dot(z_scr[...], wob_ref[...], preferred_element_type=F32)
    m_o[...] = (ga_ref[...].astype(F32) * y_a + gb_ref[...].astype(F32) * y_b).astype(BF16)


def _oproj_kernel(m_ref, wo_ref, x_ref, o_ref):
    o_ref[...] = x_ref[...] + jnp.dot(m_ref[...], wo_ref[...], preferred_element_type=F32)


def _mix_out_split(conv_b, ln_g, ln_b, ya_in, ga, gb, w_out_a, w_out_b, w_o, layer, x, *, tm_merge, tm_out):
    t = conv_b.shape[0]
    vec = pl.BlockSpec((1, D), lambda m: (0, 0))
    wfull = pl.BlockSpec((None, D, D), lambda m: (layer, 0, 0), pipeline_mode=pl.Buffered(1))
    row = pl.BlockSpec((tm_merge, D), lambda m: (m, 0))
    merged = pl.pallas_call(
        _merge_kernel,
        grid=(t // tm_merge,),
        in_specs=[row, vec, vec, row, row, row, wfull, wfull],
        out_specs=row,
        out_shape=jax.ShapeDtypeStruct((t, D), BF16),
        scratch_shapes=[pltpu.VMEM((tm_merge, D), BF16)],
        compiler_params=_params("arbitrary"),
        name="merge",
    )(conv_b, ln_g, ln_b, ya_in, ga, gb, w_out_a, w_out_b)
    row = pl.BlockSpec((tm_out, D), lambda m: (m, 0))
    return pl.pallas_call(
        _oproj_kernel,
        grid=(t // tm_out,),
        in_specs=[row, wfull, row],
        out_specs=row,
        out_shape=jax.ShapeDtypeStruct((t, D), F32),
        compiler_params=_params("arbitrary"),
        name="oproj",
    )(merged, w_o, x)


def _mix_out(conv_b, ln_g, ln_b, ya_in, ga, gb, w_out_a, w_out_b, w_o, layer, x, *, tm):
    t = conv_b.shape[0]
    row = pl.BlockSpec((tm, D), lambda m: (m, 0))
    vec = pl.BlockSpec((1, D), lambda m: (0, 0))
    wfull = pl.BlockSpec((None, D, D), lambda m: (layer, 0, 0), pipeline_mode=pl.Buffered(1))
    return pl.pallas_call(
        _mix_out_kernel,
        grid=(t // tm,),
        in_specs=[row, vec, vec, row, row, row, wfull, wfull, wfull, row],
        out_specs=row,
        out_shape=jax.ShapeDtypeStruct((t, D), F32),
        scratch_shapes=[pltpu.VMEM((tm, D), BF16)],
        compiler_params=_params("arbitrary"),
        name="mix_out",
    )(conv_b, ln_g, ln_b, ya_in, ga, gb, w_out_a, w_out_b, w_o, x)


def _ffn_tile(x_ref, g_ref, wg_ref, wu_ref, wd_ref, gf_ref, o_ref, h_scr, final_norm):
    j = pl.program_id(1)

    @pl.when(j == 0)
    def _():
        x = x_ref[...]
        h_scr[...] = _rms(x, g_ref[...]).astype(BF16)
        o_ref[...] = x

    h = h_scr[...]
    gate = jnp.dot(h, wg_ref[...].astype(BF16), preferred_element_type=F32)
    up = jnp.dot(h, wu_ref[...].astype(BF16), preferred_element_type=F32)
    act = (gate * jax.nn.sigmoid(gate) * up).astype(BF16)
    o_ref[...] += jnp.dot(act, wd_ref[...].astype(BF16), preferred_element_type=F32)

    if final_norm:
        @pl.when(j == pl.num_programs(1) - 1)
        def _():
            o_ref[...] = _rms(o_ref[...], gf_ref[...])


def _ffn_kernel(x_ref, g_ref, wg_ref, wu_ref, wd_ref, gf_ref, o_ref, h_scr):
    _ffn_tile(x_ref, g_ref, wg_ref, wu_ref, wd_ref, gf_ref, o_ref, h_scr, False)


def _ffn_final_kernel(x_ref, g_ref, wg_ref, wu_ref, wd_ref, gf_ref, op_ref, os_ref, h_scr, *, n_prompt_tiles):
    m = pl.program_id(0)

    @pl.when(m < n_prompt_tiles)
    def _():
        _ffn_tile(x_ref, g_ref, wg_ref, wu_ref, wd_ref, gf_ref, op_ref, h_scr, True)

    @pl.when(m >= n_prompt_tiles)
    def _():
        _ffn_tile(x_ref, g_ref, wg_ref, wu_ref, wd_ref, gf_ref, os_ref, h_scr, True)


def _ffn(x, g, w_gate, w_up, w_down, layer, g_final, *, tm, tf, n_prompt=None):
    t = x.shape[0]
    vec = pl.BlockSpec((1, D), lambda m, j: (0, 0))
    if n_prompt is None:
        body = _ffn_kernel
        out_specs = pl.BlockSpec((tm, D), lambda m, j: (m, 0))
        out_shape = jax.ShapeDtypeStruct((t, D), F32)
    else:
        npt = n_prompt // tm
        body = functools.partial(_ffn_final_kernel, n_prompt_tiles=npt)
        out_specs = [pl.BlockSpec((tm, D), lambda m, j: (jnp.minimum(m, npt - 1), 0)),
                     pl.BlockSpec((tm, D), lambda m, j: (jnp.maximum(m - npt, 0), 0),
                                  pipeline_mode=pl.Buffered(1))]
        out_shape = [jax.ShapeDtypeStruct((n_prompt, D), F32), jax.ShapeDtypeStruct((t - n_prompt, D), F32)]
    return pl.pallas_call(
        body,
        grid=(t // tm, D_FF // tf),
        in_specs=[pl.BlockSpec((tm, D), lambda m, j: (m, 0)), vec,
                  pl.BlockSpec((None, D, tf), lambda m, j: (layer, 0, j)),
                  pl.BlockSpec((None, D, tf), lambda m, j: (layer, 0, j)),
                  pl.BlockSpec((None, tf, D), lambda m, j: (layer, j, 0)),
                  vec],
        out_specs=out_specs,
        out_shape=out_shape,
        scratch_shapes=[pltpu.VMEM((tm, D), BF16)],
        compiler_params=_params("arbitrary", "arbitrary"),
        name="ffn",
    )(x, g, w_gate, w_up, w_down, g_final)


def kernel(x_prompt, x_sample, state_conv_a, state_conv_b, norm_mix_g, w_in, conv_a_w, w_out_a, conv_b_w,
           conv_b_bias, ln_b_g, ln_b_b, w_out_b, w_o, norm_ffn_g, w_gate, w_up, w_down, final_norm_g):
    n_seq, seq, _ = x_prompt.shape
    n_batch, n_dec, _ = x_sample.shape
    depth = w_in.shape[0]
    n_prompt = n_seq * seq
    n_sample = n_batch * n_dec
    seg = seq // SUBLANES
    shape = dict(n_seq=n_seq, seq=seq, n_batch=n_batch)

    x = jnp.concatenate(
        [x_prompt.reshape(n_seq, SUBLANES, seg, D).transpose(0, 2, 1, 3).reshape(n_prompt, D),
         x_sample.transpose(1, 0, 2).reshape(n_sample, D)], axis=0)
    w_in, w_out_a, w_out_b, w_o = (w.astype(BF16) for w in (w_in, w_out_a, w_out_b, w_o))
    g_final = final_norm_g.reshape(1, D)
    pa, pb, sa, sb = [], [], [], []
    for l in range(depth):
        vec = lambda v: v[l].reshape(1, D)
        u, ga, gb, h = _in_proj_b(x, vec(norm_mix_g), w_in, l, tm=n_sample, tn=512)
        ba, cv, conv_b, new_sb = _in_proj_a(
            h, w_in, l, u, state_conv_b[l].transpose(1, 0, 2), conv_b_w[l], vec(conv_b_bias),
            tm=n_sample, tn=256, **shape)
        ya_in, new_pa, new_pb, new_sa = _conv_a(cv, ba, u, state_conv_a[l].transpose(1, 0, 2), conv_a_w[l],
                                                tc=n_sample, cb=1024, **shape)
        x = _mix_out_split(conv_b, vec(ln_b_g), vec(ln_b_b), ya_in, ga, gb, w_out_a, w_out_b, w_o, l, x,
                           tm_merge=512, tm_out=1024)
        x = _ffn(x, vec(norm_ffn_g), w_gate, w_up, w_down, l, g_final,
                 tm=1024, tf=256, n_prompt=(n_prompt if l == depth - 1 else None))
        pa.append(new_pa)
        pb.append(new_pb)
        sa.append(new_sa.transpose(1, 0, 2))
        sb.append(new_sb.transpose(1, 0, 2))

    y_p, y_s = x
    y_prompt = y_p.reshape(n_seq, seg, SUBLANES, D).transpose(0, 2, 1, 3).reshape(n_seq, seq, D)
    y_sample = y_s.reshape(n_dec, n_batch, D).transpose(1, 0, 2)
    return (y_prompt, y_sample, jnp.stack(pa), jnp.stack(pb), jnp.stack(sa), jnp.stack(sb))
```
